```python
import math
import jax
import jax.numpy as jnp
from jax import lax
import numpy as np

D_MODEL = 1024
BATCH = 2
SEQ = 16384
DEPTH = 4
DEC_BATCH = 16
DEC_SEQ = 64
PAST_LEN = 2048

CHUNK = 64
N_A_LAYERS = DEPTH // 2
N_B_LAYERS = DEPTH - N_A_LAYERS
SGU_CHUNK = 128
SGU_GROUPS = 4
D_SGU = 2 * D_MODEL
HEAD_DIM = 64
N_HEADS_B = D_MODEL // (2 * HEAD_DIM)
QK_W = 2 * N_HEADS_B * HEAD_DIM
V_W = N_HEADS_B * 2 * HEAD_DIM
Q_BLOCK = 128
NUM_BUCKETS = 32
MAX_DISTANCE = 128
D_FF = 2816
N_EXPERTS = 8
TOP_K = 2
N_DENSE = (DEPTH + 1) // 2
N_MOE = DEPTH // 2
ALPHA = (2 * DEPTH) ** 0.25
BETA = (8 * DEPTH) ** -0.25
LN_EPS = 1e-5
NEG_INF = -1e30

kernel_name = 'yoco_gmlp_diffattn_streaming_step'


def layer_norm(x, g, b):
    xf = x.astype(jnp.float32)
    mu = jnp.mean(xf, axis=-1, keepdims=True)
    var = jnp.mean(jnp.square(xf - mu), axis=-1, keepdims=True)
    return ((xf - mu) * lax.rsqrt(var + LN_EPS) * g + b).astype(x.dtype)


def rms_norm(x, g):
    xf = x.astype(jnp.float32)
    return (xf * lax.rsqrt(jnp.mean(jnp.square(xf), axis=-1, keepdims=True) + LN_EPS) * g).astype(x.dtype)


def swiglu(x, w_gu, w_down):
    g, u = jnp.split(x @ w_gu, 2, axis=-1)
    return (jax.nn.silu(g) * u) @ w_down


def moe_ffn(x, w_router, w_gu, w_down):
    b, t, d = x.shape
    xt = x.reshape(b * t, d)
    logits = (xt @ w_router).astype(jnp.float32)
    top_val, top_idx = lax.top_k(logits, TOP_K)
    gates = jax.nn.softmax(top_val, axis=-1)
    comb = jnp.sum(jax.nn.one_hot(top_idx, N_EXPERTS, dtype=jnp.float32) * gates[..., None], axis=1)
    y = jnp.zeros_like(xt)
    for e in range(N_EXPERTS):
        y = y + comb[:, e:e + 1].astype(x.dtype) * swiglu(xt, w_gu[e], w_down[e])
    return y.reshape(b, t, d)


def sgu_mixer(x, w_in, ln_g, ln_b, w_s, b_s, w_out):
    bsz, t, _ = x.shape
    u, v = jnp.split(jax.nn.gelu(x @ w_in, approximate=False), 2, axis=-1)
    v = layer_norm(v, ln_g, ln_b)
    L = min(t, SGU_CHUNK)
    n = t // L
    w = w_s[:, :L, :L] * jnp.tril(jnp.ones((L, L), w_s.dtype))
    vb = v.reshape(bsz, n, L, SGU_GROUPS, D_SGU // SGU_GROUPS)
    mixed = jnp.einsum('gts,bnsgc->bntgc', w, vb) + b_s[:, :L].T[None, None, :, :, None]
    y = (u * mixed.reshape(bsz, t, D_SGU)) @ w_out
    return y, v


def t5_bucket(rel):
    nb = NUM_BUCKETS // 2
    max_exact = nb // 2
    ret = (rel > 0).astype(jnp.int32) * nb
    n = jnp.abs(rel)
    nf = jnp.maximum(n, 1).astype(jnp.float32)
    large = max_exact + (jnp.log(nf / max_exact) / math.log(MAX_DISTANCE / max_exact) * (nb - max_exact)).astype(jnp.int32)
    large = jnp.minimum(large, nb - 1)
    return ret + jnp.where(n < max_exact, n, large)


def diff_attend(q, k, v, q_pos, k_pos, lam, rel_bias, visible):
    s = jnp.einsum('bqhd,bkhd->bhqk', q, k).astype(jnp.float32)
    bias = rel_bias[t5_bucket(k_pos[None, :] - q_pos[:, None])]
    s = s + jnp.transpose(bias, (2, 0, 1))[None].astype(jnp.float32)
    if visible is not None:
        s = jnp.where(visible, s, NEG_INF)
    p = jax.nn.softmax(s, axis=-1)
    bsz, _, nq, nk = p.shape
    p = p.reshape(bsz, N_HEADS_B, 2, nq, nk)
    a = (p[:, :, 0] - lam * p[:, :, 1]).astype(v.dtype)
    return jnp.einsum('bhqk,bkhe->bqhe', a, v)


def diff_attn_prompt_blocks(q, k, v, lam, rel_bias):
    bsz, s_len = q.shape[:2]
    k_pos = jnp.arange(s_len)

    def one_block(i):
        q0 = i * Q_BLOCK
        qb = lax.dynamic_slice_in_dim(q, q0, Q_BLOCK, axis=1)
        q_pos = q0 + jnp.arange(Q_BLOCK)
        visible = k_pos[None, :] < (q_pos[:, None] // CHUNK + 1) * CHUNK
        return diff_attend(qb, k, v, q_pos, k_pos, lam, rel_bias, visible)

    o = lax.map(one_block, jnp.arange(s_len // Q_BLOCK))
    return jnp.transpose(o, (1, 0, 2, 3, 4)).reshape(bsz, s_len, N_HEADS_B, 2 * HEAD_DIM)


def diff_mixer(x, k_all, v_all, past_len, w_q, lam_p, subln_g, w_out, rel_bias, layer_idx):
    bsz, t, _ = x.shape
    q = (x @ w_q).reshape(bsz, t, 2 * N_HEADS_B, HEAD_DIM) * (HEAD_DIM ** -0.5)
    lam_init = 0.8 - 0.6 * math.exp(-0.3 * layer_idx)
    lp = lam_p.astype(jnp.float32)
    lam = jnp.exp(jnp.sum(lp[0] * lp[1])) - jnp.exp(jnp.sum(lp[2] * lp[3])) + lam_init
    if past_len == 0:
        o = diff_attn_prompt_blocks(q, k_all, v_all, lam, rel_bias)
    else:
        q_pos = past_len + jnp.arange(t)
        k_pos = jnp.arange(past_len + t)
        o = diff_attend(q, k_all, v_all, q_pos, k_pos, lam, rel_bias, None)
    o = rms_norm(o, subln_g) * (1.0 - lam_init)
    return o.reshape(bsz, t, V_W) @ w_out


def trunk(x, cache_k, cache_v, a_w_in, a_ln_g, a_ln_b, a_w_s, a_b_s, a_w_out, w_kv, b_w_q, b_lam, b_subln_g, b_w_out, rel_bias, ln_mix_g, ln_mix_b, ln_ffn_g, ln_ffn_b, ffn_w_gu, ffn_w_down, moe_w_router, moe_w_gu, moe_w_down):
    bsz, t, _ = x.shape
    past_len = 0 if cache_k is None else cache_k.shape[1]
    sgu_rows = []
    k_new = v_new = k_all = v_all = None
    for i in range(DEPTH):
        if i < N_A_LAYERS:
            h, v_rows = sgu_mixer(x, a_w_in[i], a_ln_g[i], a_ln_b[i], a_w_s[i], a_b_s[i], a_w_out[i])
            sgu_rows.append(v_rows)
        else:
            if k_new is None:
                kv = x @ w_kv
                k_new = kv[..., :QK_W].reshape(bsz, t, 2 * N_HEADS_B, HEAD_DIM)
                v_new = kv[..., QK_W:].reshape(bsz, t, N_HEADS_B, 2 * HEAD_DIM)
                if cache_k is None:
                    k_all, v_all = k_new, v_new
                else:
                    k_all = jnp.concatenate([cache_k, k_new], axis=1)
                    v_all = jnp.concatenate([cache_v, v_new], axis=1)
            j = i - N_A_LAYERS
            h = diff_mixer(x, k_all, v_all, past_len, b_w_q[j], b_lam[j], b_subln_g[j], b_w_out[j], rel_bias, i)
        x = layer_norm(ALPHA * x + h, ln_mix_g[i], ln_mix_b[i])
        if i % 2 == 0:
            f = swiglu(x, ffn_w_gu[i // 2], ffn_w_down[i // 2])
        else:
            f = moe_ffn(x, moe_w_router[i // 2], moe_w_gu[i // 2], moe_w_down[i // 2])
        x = layer_norm(ALPHA * x + f, ln_ffn_g[i], ln_ffn_b[i])
    return x, k_new, v_new, sgu_rows


def setup_inputs(seed: int = 0) -> dict:
    key = jax.random.key(seed)
    ks = jax.random.split(key, 26)

    def nrm(k, shape, scale):
        return scale * jax.random.normal(k, shape, jnp.float32)

    return {
        'x_prompt': nrm(ks[0], (BATCH, SEQ, D_MODEL), 1.0),
        'x_sample': nrm(ks[1], (DEC_BATCH, DEC_SEQ, D_MODEL), 1.0),
        'cache_k': nrm(ks[2], (DEC_BATCH, PAST_LEN, 2 * N_HEADS_B, HEAD_DIM), 1.0),
        'cache_v': nrm(ks[3], (DEC_BATCH, PAST_LEN, N_HEADS_B, 2 * HEAD_DIM), 1.0),
        'a_w_in': nrm(ks[4], (N_A_LAYERS, D_MODEL, 2 * D_SGU), D_MODEL ** -0.5),
        'a_ln_g': 1.0 + nrm(ks[5], (N_A_LAYERS, D_SGU), 0.02),
        'a_ln_b': nrm(ks[6], (N_A_LAYERS, D_SGU), 0.02),
        'a_w_s': nrm(ks[7], (N_A_LAYERS, SGU_GROUPS, SGU_CHUNK, SGU_CHUNK), SGU_CHUNK ** -0.5),
        'a_b_s': 1.0 + nrm(ks[8], (N_A_LAYERS, SGU_GROUPS, SGU_CHUNK), 0.02),
        'a_w_out': nrm(ks[9], (N_A_LAYERS, D_SGU, D_MODEL), BETA * D_SGU ** -0.5),
        'w_kv': nrm(ks[10], (D_MODEL, QK_W + V_W), D_MODEL ** -0.5),
        'b_w_q': nrm(ks[11], (N_B_LAYERS, D_MODEL, QK_W), D_MODEL ** -0.5),
        'b_lam': nrm(ks[12], (N_B_LAYERS, 4, HEAD_DIM), 0.1),
        'b_subln_g': 1.0 + nrm(ks[13], (N_B_LAYERS, 2 * HEAD_DIM), 0.02),
        'b_w_out': nrm(ks[14], (N_B_LAYERS, V_W, D_MODEL), BETA * V_W ** -0.5),
        'rel_bias': nrm(ks[15], (NUM_BUCKETS, 2 * N_HEADS_B), 0.5),
        'ln_mix_g': 1.0 + nrm(ks[16], (DEPTH, D_MODEL), 0.02),
        'ln_mix_b': nrm(ks[17], (DEPTH, D_MODEL), 0.02),
        'ln_ffn_g': 1.0 + nrm(ks[18], (DEPTH, D_MODEL), 0.02),
        'ln_ffn_b': nrm(ks[19], (DEPTH, D_MODEL), 0.02),
        'ffn_w_gu': nrm(ks[20], (N_DENSE, D_MODEL, 2 * D_FF), D_MODEL ** -0.5),
        'ffn_w_down': nrm(ks[21], (N_DENSE, D_FF, D_MODEL), BETA * D_FF ** -0.5),
        'moe_w_router': nrm(ks[22], (N_MOE, D_MODEL, N_EXPERTS), D_MODEL ** -0.5),
        'moe_w_gu': nrm(ks[23], (N_MOE, N_EXPERTS, D_MODEL, 2 * D_FF), D_MODEL ** -0.5),
        'moe_w_down': nrm(ks[24], (N_MOE, N_EXPERTS, D_FF, D_MODEL), BETA * D_FF ** -0.5),
    }


def reference(x_prompt, x_sample, cache_k, cache_v, a_w_in, a_ln_g, a_ln_b, a_w_s, a_b_s, a_w_out, w_kv, b_w_q, b_lam, b_subln_g, b_w_out, rel_bias, ln_mix_g, ln_mix_b, ln_ffn_g, ln_ffn_b, ffn_w_gu, ffn_w_down, moe_w_router, moe_w_gu, moe_w_down):
    y_prompt, k_prompt, v_prompt, _ = trunk(
        x_prompt, None, None, a_w_in, a_ln_g, a_ln_b, a_w_s, a_b_s, a_w_out, w_kv, b_w_q, b_lam, b_subln_g, b_w_out, rel_bias,
        ln_mix_g, ln_mix_b, ln_ffn_g, ln_ffn_b, ffn_w_gu, ffn_w_down, moe_w_router, moe_w_gu, moe_w_down)
    y_sample, k_sample, v_sample, sgu_rows_sample = trunk(
        x_sample, cache_k, cache_v, a_w_in, a_ln_g, a_ln_b, a_w_s, a_b_s, a_w_out, w_kv, b_w_q, b_lam, b_subln_g, b_w_out, rel_bias,
        ln_mix_g, ln_mix_b, ln_ffn_g, ln_ffn_b, ffn_w_gu, ffn_w_down, moe_w_router, moe_w_gu, moe_w_down)
    sgu_v_sample = jnp.stack(sgu_rows_sample, axis=0)
    return (y_prompt, y_sample, k_prompt, v_prompt, k_sample, v_sample, sgu_v_sample)
```

```python
import functools
import math

import jax
import jax.numpy as jnp
from jax import lax
from jax.experimental import pallas as pl
from jax.experimental.pallas import tpu as pltpu

F32 = jnp.float32
BF16 = jnp.bfloat16

CHUNK = 64
SGU_CHUNK = 128
SGU_GROUPS = 4
HEAD_DIM = 64
NUM_BUCKETS = 32
MAX_DISTANCE = 128
N_EXPERTS = 8
LN_EPS = 1e-5
NEG_INF = -1e30
SQRT_HALF = math.sqrt(0.5)

LANES = 128
VMEM_LIMIT_BYTES = 56 * 1024 * 1024

SGU_ROWS = 256
FFN_ROWS = 512
FFN_SPLIT = 2
ROUTE_ROWS = 1024
MOVE_ROWS = 256
EXPERT_ROWS = 512
PROJ_ROWS = 512
ATT_Q = 256
ATT_K = 256
CACHE_K = 512


def _cparams(semantics):
    return pltpu.CompilerParams(dimension_semantics=semantics, vmem_limit_bytes=VMEM_LIMIT_BYTES)


def _resident(shape):
    nd = len(shape)
    return pl.BlockSpec(shape, lambda *_: (0,) * nd, pipeline_mode=pl.Buffered(1))


def _layer_norm(z, g, b):
    mu = jnp.mean(z, axis=-1, keepdims=True)
    zc = z - mu
    var = jnp.mean(zc * zc, axis=-1, keepdims=True)
    return zc * lax.rsqrt(var + LN_EPS) * g + b


def _gelu_exact(h):
    return 0.5 * h * (1.0 + lax.erf(h * SQRT_HALF))


def _sgu_kernel(x_ref, win_ref, lg_ref, lb_ref, ws_ref, bs_ref, wout_ref, mg_ref, mb_ref,
                o_ref, *rest, chunk, alpha, emit_v):
    if emit_v:
        v_ref, gate_ref = rest
    else:
        (gate_ref,) = rest
    rows = x_ref.shape[0]
    d_sgu = wout_ref.shape[0]
    gw = d_sgu // SGU_GROUPS
    x = x_ref[...]
    xb = x.astype(BF16)
    v = _gelu_exact(jnp.dot(xb, win_ref[:, d_sgu:], preferred_element_type=F32))
    vn = _layer_norm(v, lg_ref[...], lb_ref[...])
    if emit_v:
        v_ref[...] = vn
    vb = vn.astype(BF16)
    u = _gelu_exact(jnp.dot(xb, win_ref[:, :d_sgu], preferred_element_type=F32))
    tril = (lax.broadcasted_iota(jnp.int32, (chunk, chunk), 0)
            >= lax.broadcasted_iota(jnp.int32, (chunk, chunk), 1))
    for g in range(SGU_GROUPS):
        wg = jnp.where(tril, ws_ref[g], 0.0).astype(BF16)
        bcol = bs_ref[:, g:g + 1]
        for c in range(rows // chunk):
            r0, r1, c0, c1 = c * chunk, (c + 1) * chunk, g * gw, (g + 1) * gw
            mixed = jnp.dot(wg, vb[r0:r1, c0:c1], preferred_element_type=F32) + bcol
            gate_ref[r0:r1, c0:c1] = (u[r0:r1, c0:c1] * mixed).astype(BF16)
    y = jnp.dot(gate_ref[...], wout_ref[...], preferred_element_type=F32)
    o_ref[...] = _layer_norm(alpha * x + y, mg_ref[...], mb_ref[...])


def _sgu_layer(x, w_in, ln_g, ln_b, w_s, b_s, w_out, mix_g, mix_b, *, chunk, alpha, emit_v):
    n, d = x.shape
    d_sgu = w_out.shape[0]
    rows = min(SGU_ROWS, n)
    assert n % rows == 0 and rows % chunk == 0
    row_spec = pl.BlockSpec((rows, d), lambda i: (i, 0))
    out_shape = [jax.ShapeDtypeStruct((n, d), F32)]
    out_specs = [row_spec]
    if emit_v:
        out_shape.append(jax.ShapeDtypeStruct((n, d_sgu), F32))
        out_specs.append(pl.BlockSpec((rows, d_sgu), lambda i: (i, 0)))
    res = pl.pallas_call(
        functools.partial(_sgu_kernel, chunk=chunk, alpha=alpha, emit_v=emit_v),
        grid=(n // rows,),
        in_specs=[row_spec, _resident(w_in.shape), _resident(ln_g.shape), _resident(ln_b.shape),
                  _resident(w_s.shape), _resident(b_s.shape), _resident(w_out.shape),
                  _resident(mix_g.shape), _resident(mix_b.shape)],
        out_specs=out_specs,
        out_shape=out_shape,
        scratch_shapes=[pltpu.VMEM((rows, d_sgu), BF16)],
        name="sgu_layer",
        compiler_params=_cparams(("parallel",)),
    )(x, w_in, ln_g, ln_b, w_s, b_s, w_out, mix_g, mix_b)
    return res if emit_v else (res[0], None)


def _swiglu_piece(xb, wg, wu, wd):
    g = jnp.dot(xb, wg, preferred_element_type=F32)
    u = jnp.dot(xb, wu, preferred_element_type=F32)
    a = (g * jax.nn.sigmoid(g) * u).astype(BF16)
    return jnp.dot(a, wd, preferred_element_type=F32)


def _ffn_kernel(x_ref, wgu_ref, wd_ref, g_ref, b_ref, o_ref, *, alpha):
    d_ff = wd_ref.shape[0]
    piece = d_ff // FFN_SPLIT
    x = x_ref[...]
    xb = x.astype(BF16)
    acc = None
    for j in range(FFN_SPLIT):
        t = _swiglu_piece(xb, wgu_ref[:, j * piece:(j + 1) * piece],
                          wgu_ref[:, d_ff + j * piece:d_ff + (j + 1) * piece],
                          wd_ref[j * piece:(j + 1) * piece, :])
        acc = t if acc is None else acc + t
    o_ref[...] = _layer_norm(alpha * x + acc, g_ref[...], b_ref[...])


def _ffn_layer(x, w_gu, w_down, ln_g, ln_b, *, alpha):
    n, d = x.shape
    rows = min(FFN_ROWS, n)
    assert n % rows == 0 and w_down.shape[0] % (FFN_SPLIT * LANES) == 0
    row_spec = pl.BlockSpec((rows, d), lambda i: (i, 0))
    return pl.pallas_call(
        functools.partial(_ffn_kernel, alpha=alpha),
        grid=(n // rows,),
        in_specs=[row_spec, _resident(w_gu.shape), _resident(w_down.shape),
                  _resident(ln_g.shape), _resident(ln_b.shape)],
        out_specs=row_spec,
        out_shape=jax.ShapeDtypeStruct((n, d), F32),
        name="dense_ffn",
        compiler_params=_cparams(("parallel",)),
    )(x, w_gu, w_down, ln_g, ln_b)


def _router_kernel(x_ref, wr_ref, idx_ref, gate_ref):
    logits = lax.dot_general(wr_ref[...], x_ref[...], (((1,), (1,)), ((), ())),
                             precision=lax.Precision.HIGHEST, preferred_element_type=F32)
    ids = lax.broadcasted_iota(jnp.int32, logits.shape, 0)
    m1 = jnp.max(logits, axis=0, keepdims=True)
    i1 = jnp.min(jnp.where(logits == m1, ids, N_EXPERTS), axis=0, keepdims=True)
    rest = jnp.where(ids == i1, -jnp.inf, logits)
    m2 = jnp.max(rest, axis=0, keepdims=True)
    i2 = jnp.min(jnp.where(rest == m2, ids, N_EXPERTS), axis=0, keepdims=True)
    e2 = jnp.exp(m2 - m1)
    den = 1.0 + e2
    idx_ref[...] = jnp.concatenate([i1, i2], axis=0)
    gate_ref[...] = jnp.concatenate([1.0 / den, e2 / den], axis=0)


def _router(x, w_router_t):
    n, d = x.shape
    rows = min(ROUTE_ROWS, n)
    assert n % rows == 0
    return pl.pallas_call(
        _router_kernel,
        grid=(n // rows,),
        in_specs=[pl.BlockSpec((rows, d), lambda i: (i, 0)), _resident(w_router_t.shape)],
        out_specs=[pl.BlockSpec((2, rows), lambda i: (0, i)), pl.BlockSpec((2, rows), lambda i: (0, i))],
        out_shape=[jax.ShapeDtypeStruct((2, n), jnp.int32), jax.ShapeDtypeStruct((2, n), F32)],
        name="moe_router",
        compiler_params=_cparams(("parallel",)),
    )(x, w_router_t)


def _route_plan(idx, n_rows_padded):
    e_flat = idx.reshape(-1)
    onehot = (e_flat[:, None] == jnp.arange(N_EXPERTS, dtype=jnp.int32)[None, :]).astype(jnp.int32)
    csum = jnp.cumsum(onehot, axis=0)
    counts = csum[-1]
    padded = ((counts + EXPERT_ROWS - 1) // EXPERT_ROWS) * EXPERT_ROWS
    ends = jnp.cumsum(padded)
    starts = ends - padded
    dest = jnp.sum(onehot * (starts[None, :] + csum - 1), axis=1)
    tile_start = jnp.arange(n_rows_padded // EXPERT_ROWS, dtype=jnp.int32) * EXPERT_ROWS
    tile_expert = jnp.sum((tile_start[:, None] >= ends[None, :]).astype(jnp.int32), axis=1)
    return dest.astype(jnp.int32), jnp.minimum(tile_expert, N_EXPERTS - 1).astype(jnp.int32)


def _row_copy(src_ref, src_row, dst_ref, dst_row, sem):
    return pltpu.make_async_copy(src_ref.at[pl.ds(src_row, 1)], dst_ref.at[pl.ds(dst_row, 1)], sem)


def _dispatch_kernel(dest_ref, x_ref, xs_in_ref, xs_ref, sem):
    del xs_in_ref
    rows = x_ref.shape[0]

    def start(r, carry):
        _row_copy(x_ref, r, xs_ref, dest_ref[0, r], sem).start()
        _row_copy(x_ref, r, xs_ref, dest_ref[0, rows + r], sem).start()
        return carry

    def wait(r, carry):
        _row_copy(x_ref, 0, xs_ref, 0, sem).wait()
        _row_copy(x_ref, 0, xs_ref, 0, sem).wait()
        return carry

    lax.fori_loop(0, rows, start, 0)
    lax.fori_loop(0, rows, wait, 0)


def _dispatch(x, dest_tiles, n_rows_padded):
    n, d = x.shape
    rows = dest_tiles.shape[-1] // 2
    zeros = jnp.zeros((n_rows_padded, d), F32)
    return pl.pallas_call(
        _dispatch_kernel,
        grid=(n // rows,),
        in_specs=[pl.BlockSpec((None, 1, 2 * rows), lambda i: (i, 0, 0), memory_space=pltpu.SMEM),
                  pl.BlockSpec((rows, d), lambda i: (i, 0)),
                  pl.BlockSpec(memory_space=pl.ANY)],
        out_specs=pl.BlockSpec(memory_space=pl.ANY),
        out_shape=jax.ShapeDtypeStruct((n_rows_padded, d), F32),
        scratch_shapes=[pltpu.SemaphoreType.DMA(())],
        input_output_aliases={2: 0},
        name="moe_dispatch",
        compiler_params=_cparams(("arbitrary",)),
    )(dest_tiles, x, zeros)


def _expert_kernel(te_ref, xs_ref, wg_ref, wu_ref, wd_ref, o_ref):
    del te_ref
    j = pl.program_id(1)
    t = _swiglu_piece(xs_ref[...].astype(BF16), wg_ref[...], wu_ref[...], wd_ref[...])

    @pl.when(j == 0)
    def _():
        o_ref[...] = t

    @pl.when(j > 0)
    def _():
        o_ref[...] += t


def _experts(xs, tile_expert, w_gu, w_down):
    r, d = xs.shape
    d_ff = w_down.shape[1]
    piece = d_ff // FFN_SPLIT
    grid_spec = pltpu.PrefetchScalarGridSpec(
        num_scalar_prefetch=1,
        grid=(r // EXPERT_ROWS, FFN_SPLIT),
        in_specs=[pl.BlockSpec((EXPERT_ROWS, d), lambda t, j, te: (t, 0)),
                  pl.BlockSpec((None, d, piece), lambda t, j, te: (te[t], 0, j)),
                  pl.BlockSpec((None, d, piece), lambda t, j, te: (te[t], 0, FFN_SPLIT + j)),
                  pl.BlockSpec((None, piece, d), lambda t, j, te: (te[t], j, 0))],
        out_specs=pl.BlockSpec((EXPERT_ROWS, d), lambda t, j, te: (t, 0)),
    )
    return pl.pallas_call(
        _expert_kernel,
        grid_spec=grid_spec,
        out_shape=jax.ShapeDtypeStruct((r, d), F32),
        name="moe_experts",
        compiler_params=_cparams(("parallel", "arbitrary")),
    )(tile_expert, xs, w_gu, w_gu, w_down)


def _combine_kernel(dest_ref, x_ref, gate_ref, ys_ref, g_ref, b_ref, o_ref, buf_ref, sem, *, alpha):
    rows = x_ref.shape[0]

    def start(r, carry):
        _row_copy(ys_ref, dest_ref[0, r], buf_ref.at[0], r, sem).start()
        _row_copy(ys_ref, dest_ref[0, rows + r], buf_ref.at[1], r, sem).start()
        return carry

    def wait(r, carry):
        _row_copy(ys_ref, 0, buf_ref.at[0], 0, sem).wait()
        _row_copy(ys_ref, 0, buf_ref.at[1], 0, sem).wait()
        return carry

    lax.fori_loop(0, rows, start, 0)
    lax.fori_loop(0, rows, wait, 0)
    gates = gate_ref[...]
    y = gates[:, 0:1] * buf_ref[0] + gates[:, 1:2] * buf_ref[1]
    o_ref[...] = _layer_norm(alpha * x_ref[...] + y, g_ref[...], b_ref[...])


def _combine(x, dest_tiles, gates_t, ys, ln_g, ln_b, *, alpha):
    n, d = x.shape
    rows = dest_tiles.shape[-1] // 2
    row_spec = pl.BlockSpec((rows, d), lambda i: (i, 0))
    return pl.pallas_call(
        functools.partial(_combine_kernel, alpha=alpha),
        grid=(n // rows,),
        in_specs=[pl.BlockSpec((None, 1, 2 * rows), lambda i: (i, 0, 0), memory_space=pltpu.SMEM),
                  row_spec,
                  pl.BlockSpec((rows, 2), lambda i: (i, 0)),
                  pl.BlockSpec(memory_space=pl.ANY),
                  _resident(ln_g.shape), _resident(ln_b.shape)],
        out_specs=row_spec,
        out_shape=jax.ShapeDtypeStruct((n, d), F32),
        scratch_shapes=[pltpu.VMEM((2, rows, d), F32), pltpu.SemaphoreType.DMA(())],
        name="moe_combine",
        compiler_params=_cparams(("arbitrary",)),
    )(dest_tiles, x, gates_t, ys, ln_g, ln_b)


def _moe_layer(x, w_router_t, w_gu, w_down, ln_g, ln_b, *, alpha):
    n, _ = x.shape
    rows = min(MOVE_ROWS, n)
    assert n % rows == 0
    n_rows_padded = 2 * n + N_EXPERTS * EXPERT_ROWS
    idx, gates = _router(x, w_router_t)
    dest, tile_expert = _route_plan(idx, n_rows_padded)
    dest_tiles = dest.reshape(2, n // rows, rows).transpose(1, 0, 2).reshape(n // rows, 1, 2 * rows)
    xs = _dispatch(x, dest_tiles, n_rows_padded)
    ys = _experts(xs, tile_expert, w_gu, w_down)
    return _combine(x, dest_tiles, gates.T, ys, ln_g, ln_b, alpha=alpha)


def _kv_kernel(x_ref, w_ref, k_ref, v_ref, kb_ref, vb_ref):
    qk_w = k_ref.shape[1]
    kv = jnp.dot(x_ref[...].astype(BF16), w_ref[...], preferred_element_type=F32)
    k = kv[:, :qk_w]
    v = kv[:, qk_w:]
    k_ref[...] = k
    v_ref[...] = v
    kb_ref[...] = k.astype(BF16)
    vb_ref[...] = v.astype(BF16)


def _kv_proj(x, w_kv):
    n, d = x.shape
    qk_w = w_kv.shape[1] // 2
    rows = min(PROJ_ROWS, n)
    assert n % rows == 0
    out_spec = pl.BlockSpec((rows, qk_w), lambda i: (i, 0))
    return pl.pallas_call(
        _kv_kernel,
        grid=(n // rows,),
        in_specs=[pl.BlockSpec((rows, d), lambda i: (i, 0)), _resident(w_kv.shape)],
        out_specs=[out_spec] * 4,
        out_shape=[jax.ShapeDtypeStruct((n, qk_w), F32)] * 2 + [jax.ShapeDtypeStruct((n, qk_w), BF16)] * 2,
        name="kv_proj",
        compiler_params=_cparams(("parallel",)),
    )(x, w_kv)


def _t5_bucket(rel):
    nb = NUM_BUCKETS // 2
    max_exact = nb // 2
    ret = (rel > 0).astype(jnp.int32) * nb
    n = jnp.abs(rel)
    nf = jnp.maximum(n, 1).astype(jnp.float32)
    large = max_exact + (jnp.log(nf / max_exact) / math.log(MAX_DISTANCE / max_exact)
                         * (nb - max_exact)).astype(jnp.int32)
    large = jnp.minimum(large, nb - 1)
    return ret + jnp.where(n < max_exact, n, large)


def _bias_table(rel_bias, q_pos, k_pos, far, visible=None):
    b = rel_bias[_t5_bucket(k_pos[None, :] - q_pos[:, None])].astype(F32) - far[None, None, :]
    if visible is not None:
        b = jnp.where(visible[:, :, None], b, NEG_INF)
    return jnp.transpose(b, (2, 0, 1))


def _far_bias(rel_bias):
    return rel_bias[_t5_bucket(jnp.array(-MAX_DISTANCE, jnp.int32))].astype(F32)


def _att_init(x_ref, wq_ref, qa_ref, qb_ref, m_ref, l_ref, acc_ref):
    q = jnp.dot(x_ref[...].astype(BF16), wq_ref[...], preferred_element_type=F32) * (HEAD_DIM ** -0.5)
    lane = lax.broadcasted_iota(jnp.int32, (q.shape[0], LANES), 1)
    for h in range(qa_ref.shape[0]):
        slab = q[:, h * LANES:(h + 1) * LANES]
        qa_ref[h] = jnp.where(lane < HEAD_DIM, slab, 0.0).astype(BF16)
        qb_ref[h] = jnp.where(lane >= HEAD_DIM, slab, 0.0).astype(BF16)
    m_ref[...] = jnp.full(m_ref.shape, NEG_INF, F32)
    l_ref[...] = jnp.zeros(l_ref.shape, F32)
    acc_ref[...] = jnp.zeros(acc_ref.shape, F32)


def _att_block(qa_ref, qb_ref, k, v, bias_ref, m_ref, l_ref, acc_ref):
    for h in range(qa_ref.shape[0]):
        kp = k[:, h * LANES:(h + 1) * LANES]
        vp = v[:, h * LANES:(h + 1) * LANES]
        for t, q_ref in enumerate((qa_ref, qb_ref)):
            hh = 2 * h + t
            s = lax.dot_general(q_ref[h], kp, (((1,), (1,)), ((), ())), preferred_element_type=F32)
            if bias_ref is not None:
                s = s + bias_ref[hh]
            m_old = m_ref[hh]
            m_new = jnp.maximum(m_old, jnp.max(s, axis=-1, keepdims=True))
            scale = jnp.exp(m_old - m_new)
            p = jnp.exp(s - m_new)
            l_ref[hh] = scale * l_ref[hh] + jnp.sum(p, axis=-1, keepdims=True)
            acc_ref[hh] = scale * acc_ref[hh] + jnp.dot(p.astype(BF16), vp, preferred_element_type=F32)
            m_ref[hh] = m_new


def _att_finish(x_ref, lam_ref, sg_ref, wout_ref, g_ref, b_ref, o_ref, l_ref, acc_ref, heads_ref,
                *, lam_init, alpha):
    lp = lam_ref[...]
    lam = (jnp.exp(jnp.sum(lp[0:1] * lp[1:2], axis=-1, keepdims=True))
           - jnp.exp(jnp.sum(lp[2:3] * lp[3:4], axis=-1, keepdims=True)) + lam_init)
    for h in range(acc_ref.shape[0] // 2):
        a = acc_ref[2 * h] / l_ref[2 * h] - lam * (acc_ref[2 * h + 1] / l_ref[2 * h + 1])
        r = a * lax.rsqrt(jnp.mean(a * a, axis=-1, keepdims=True) + LN_EPS) * sg_ref[...]
        heads_ref[:, h * LANES:(h + 1) * LANES] = (r * (1.0 - lam_init)).astype(BF16)
    y = jnp.dot(heads_ref[...], wout_ref[...], preferred_element_type=F32)
    o_ref[...] = _layer_norm(alpha * x_ref[...] + y, g_ref[...], b_ref[...])


def _att_scratch(q_rows, width):
    n_slabs = width // LANES
    return [pltpu.VMEM((n_slabs, q_rows, LANES), BF16), pltpu.VMEM((n_slabs, q_rows, LANES), BF16),
            pltpu.VMEM((2 * n_slabs, q_rows, 1), F32), pltpu.VMEM((2 * n_slabs, q_rows, 1), F32),
            pltpu.VMEM((2 * n_slabs, q_rows, LANES), F32), pltpu.VMEM((q_rows, width), BF16)]


def _prompt_att_kernel(qi_ref, kj_ref, x_ref, k_ref, v_ref, wq_ref, diag_ref, sub_ref, lam_ref, sg_ref,
                       wout_ref, g_ref, b_ref, o_ref, qa_ref, qb_ref, m_ref, l_ref, acc_ref, heads_ref,
                       *, lam_init, alpha):
    p = pl.program_id(1)
    qi = qi_ref[p]
    kj = kj_ref[p]
    state = (m_ref, l_ref, acc_ref)

    @pl.when(kj == 0)
    def _():
        _att_init(x_ref, wq_ref, qa_ref, qb_ref, *state)

    @pl.when(kj < qi - 1)
    def _():
        _att_block(qa_ref, qb_ref, k_ref[...], v_ref[...], None, *state)

    @pl.when(kj == qi - 1)
    def _():
        _att_block(qa_ref, qb_ref, k_ref[...], v_ref[...], sub_ref, *state)

    @pl.when(kj == qi)
    def _():
        _att_block(qa_ref, qb_ref, k_ref[...], v_ref[...], diag_ref, *state)
        _att_finish(x_ref, lam_ref, sg_ref, wout_ref, g_ref, b_ref, o_ref, l_ref, acc_ref, heads_ref,
                    lam_init=lam_init, alpha=alpha)


def _prompt_attention(x, kb, vb, w_q, lam_p, subln_g, w_out, rel_bias, ln_g, ln_b, *, lam_init, alpha):
    bsz, s_len, d = x.shape
    width = kb.shape[-1]
    tq = min(ATT_Q, s_len)
    assert tq == ATT_K or tq == s_len
    assert s_len % tq == 0 and tq % CHUNK == 0 and tq >= MAX_DISTANCE
    n_q = s_len // tq
    pairs = [(i, j) for i in range(n_q) for j in range(i + 1)]
    qi = jnp.array([ij[0] for ij in pairs], jnp.int32)
    kj = jnp.array([ij[1] for ij in pairs], jnp.int32)
    far = _far_bias(rel_bias)
    pos = jnp.arange(tq, dtype=jnp.int32)
    visible = pos[None, :] < (pos[:, None] // CHUNK + 1) * CHUNK
    diag = _bias_table(rel_bias, pos, pos, far, visible)
    sub = _bias_table(rel_bias, pos + tq, pos, far)
    grid_spec = pltpu.PrefetchScalarGridSpec(
        num_scalar_prefetch=2,
        grid=(bsz, len(pairs)),
        in_specs=[pl.BlockSpec((None, tq, d), lambda b, p, qi, kj: (b, qi[p], 0)),
                  pl.BlockSpec((None, tq, width), lambda b, p, qi, kj: (b, kj[p], 0)),
                  pl.BlockSpec((None, tq, width), lambda b, p, qi, kj: (b, kj[p], 0)),
                  _resident(w_q.shape), _resident(diag.shape), _resident(sub.shape),
                  _resident(lam_p.shape), _resident(subln_g.shape), _resident(w_out.shape),
                  _resident(ln_g.shape), _resident(ln_b.shape)],
        out_specs=pl.BlockSpec((None, tq, d), lambda b, p, qi, kj: (b, qi[p], 0)),
        scratch_shapes=_att_scratch(tq, width),
    )
    return pl.pallas_call(
        functools.partial(_prompt_att_kernel, lam_init=lam_init, alpha=alpha),
        grid_spec=grid_spec,
        out_shape=jax.ShapeDtypeStruct((bsz, s_len, d), F32),
        name="prompt_attention",
        compiler_params=_cparams(("parallel", "arbitrary")),
    )(qi, kj, x, kb, vb, w_q, diag, sub, lam_p, subln_g, w_out, ln_g, ln_b)


def _sample_att_kernel(x_ref, ck_ref, cv_ref, kn_ref, vn_ref, wq_ref, near_ref, new_ref, lam_ref, sg_ref,
                       wout_ref, g_ref, b_ref, o_ref, qa_ref, qb_ref, m_ref, l_ref, acc_ref, heads_ref,
                       *, lam_init, alpha):
    j = pl.program_id(1)
    n_cache = pl.num_programs(1) - 1
    state = (m_ref, l_ref, acc_ref)

    @pl.when(j == 0)
    def _():
        _att_init(x_ref, wq_ref, qa_ref, qb_ref, *state)

    @pl.when(j < n_cache - 1)
    def _():
        _att_block(qa_ref, qb_ref, ck_ref[...].astype(BF16), cv_ref[...].astype(BF16), None, *state)

    @pl.when(j == n_cache - 1)
    def _():
        _att_block(qa_ref, qb_ref, ck_ref[...].astype(BF16), cv_ref[...].astype(BF16), near_ref, *state)

    @pl.when(j == n_cache)
    def _():
        _att_block(qa_ref, qb_ref, kn_ref[...], vn_ref[...], new_ref, *state)
        _att_finish(x_ref, lam_ref, sg_ref, wout_ref, g_ref, b_ref, o_ref, l_ref, acc_ref, heads_ref,
                    lam_init=lam_init, alpha=alpha)


def _sample_attention(x, cache_k, cache_v, kb, vb, w_q, lam_p, subln_g, w_out, rel_bias, ln_g, ln_b,
                      *, lam_init, alpha):
    bsz, t, d = x.shape
    past = cache_k.shape[1]
    width = kb.shape[-1]
    tk = min(CACHE_K, past)
    assert past % tk == 0 and tk >= MAX_DISTANCE
    n_cache = past // tk
    far = _far_bias(rel_bias)
    q_pos = past + jnp.arange(t, dtype=jnp.int32)
    near = _bias_table(rel_bias, q_pos, past - tk + jnp.arange(tk, dtype=jnp.int32), far)
    new = _bias_table(rel_bias, q_pos, q_pos, far)
    cache_spec = pl.BlockSpec((None, tk, width), lambda b, j: (b, jnp.minimum(j, n_cache - 1), 0))
    new_spec = pl.BlockSpec((None, t, width), lambda b, j: (b, 0, 0))
    x_spec = pl.BlockSpec((None, t, d), lambda b, j: (b, 0, 0))
    return pl.pallas_call(
        functools.partial(_sample_att_kernel, lam_init=lam_init, alpha=alpha),
        grid=(bsz, n_cache + 1),
        in_specs=[x_spec, cache_spec, cache_spec, new_spec, new_spec,
                  _resident(w_q.shape), _resident(near.shape), _resident(new.shape),
                  _resident(lam_p.shape), _resident(subln_g.shape), _resident(w_out.shape),
                  _resident(ln_g.shape), _resident(ln_b.shape)],
        out_specs=x_spec,
        out_shape=jax.ShapeDtypeStruct((bsz, t, d), F32),
        scratch_shapes=_att_scratch(t, width),
        name="sample_attention",
        compiler_params=_cparams(("parallel", "arbitrary")),
    )(x, cache_k, cache_v, kb, vb, w_q, near, new, lam_p, subln_g, w_out, ln_g, ln_b)


def _trunk(x, cache_k, cache_v, w, *, depth, emit_v):
    bsz, t, d = x.shape
    n = bsz * t
    n_a = depth // 2
    alpha = (2 * depth) ** 0.25
    chunk = min(t, SGU_CHUNK)
    h = x.reshape(n, d)
    sgu_rows = []
    k_new = v_new = kb = vb = None
    for i in range(depth):
        if i < n_a:
            h, v_rows = _sgu_layer(
                h, w["a_w_in"][i], w["a_ln_g"][i], w["a_ln_b"][i], w["a_w_s"][i][:, :chunk, :chunk],
                w["a_b_s"][i][:, :chunk].T, w["a_w_out"][i], w["ln_mix_g"][i], w["ln_mix_b"][i],
                chunk=chunk, alpha=alpha, emit_v=emit_v)
            sgu_rows.append(v_rows)
        else:
            if k_new is None:
                k_new, v_new, kb, vb = _kv_proj(h, w["w_kv"])
            j = i - n_a
            lam_init = 0.8 - 0.6 * math.exp(-0.3 * i)
            args = (w["b_w_q"][j], w["b_lam"][j], w["b_subln_g"][j], w["b_w_out"][j], w["rel_bias"],
                    w["ln_mix_g"][i], w["ln_mix_b"][i])
            if cache_k is None:
                h3 = _prompt_attention(h.reshape(bsz, t, d), kb.reshape(bsz, t, -1), vb.reshape(bsz, t, -1),
                                       *args, lam_init=lam_init, alpha=alpha)
            else:
                past = cache_k.shape[1]
                h3 = _sample_attention(h.reshape(bsz, t, d), cache_k.reshape(bsz, past, -1),
                                       cache_v.reshape(bsz, past, -1), kb.reshape(bsz, t, -1),
                                       vb.reshape(bsz, t, -1), *args, lam_init=lam_init, alpha=alpha)
            h = h3.reshape(n, d)
        if i % 2 == 0:
            h = _ffn_layer(h, w["ffn_w_gu"][i // 2], w["ffn_w_down"][i // 2],
                           w["ln_ffn_g"][i], w["ln_ffn_b"][i], alpha=alpha)
        else:
            h = _moe_layer(h, w["moe_w_router_t"][i // 2], w["moe_w_gu"][i // 2], w["moe_w_down"][i // 2],
                           w["ln_ffn_g"][i], w["ln_ffn_b"][i], alpha=alpha)
    return h.reshape(bsz, t, d), k_new, v_new, sgu_rows


def kernel(x_prompt, x_sample, cache_k, cache_v, a_w_in, a_ln_g, a_ln_b, a_w_s, a_b_s, a_w_out, w_kv, b_w_q,
           b_lam, b_subln_g, b_w_out, rel_bias, ln_mix_g, ln_mix_b, ln_ffn_g, ln_ffn_b, ffn_w_gu, ffn_w_down,
           moe_w_router, moe_w_gu, moe_w_down):
    depth = ln_mix_g.shape[0]
    n_heads2 = rel_bias.shape[1]
    row = lambda a: a[:, None, :]
    w = dict(
        a_w_in=a_w_in.astype(BF16), a_ln_g=row(a_ln_g), a_ln_b=row(a_ln_b), a_w_s=a_w_s, a_b_s=a_b_s,
        a_w_out=a_w_out.astype(BF16), w_kv=w_kv.astype(BF16), b_w_q=b_w_q.astype(BF16), b_lam=b_lam,
        b_subln_g=row(b_subln_g), b_w_out=b_w_out.astype(BF16), rel_bias=rel_bias,
        ln_mix_g=row(ln_mix_g), ln_mix_b=row(ln_mix_b), ln_ffn_g=row(ln_ffn_g), ln_ffn_b=row(ln_ffn_b),
        ffn_w_gu=ffn_w_gu.astype(BF16), ffn_w_down=ffn_w_down.astype(BF16),
        moe_w_router_t=jnp.swapaxes(moe_w_router, 1, 2), moe_w_gu=moe_w_gu.astype(BF16),
        moe_w_down=moe_w_down.astype(BF16))
    y_p, k_p, v_p, _ = _trunk(x_prompt, None, None, w, depth=depth, emit_v=False)
    y_s, k_s, v_s, sgu = _trunk(x_sample, cache_k, cache_v, w, depth=depth, emit_v=True)
    bp, sp, _ = x_prompt.shape
    bs, ss, _ = x_sample.shape
    n_heads = n_heads2 // 2
    d_sgu = a_w_out.shape[1]
    return (y_p, y_s,
            k_p.reshape(bp, sp, n_heads2, HEAD_DIM), v_p.reshape(bp, sp, n_heads, 2 * HEAD_DIM),
            k_s.reshape(bs, ss, n_heads2, HEAD_DIM), v_s.reshape(bs, ss, n_heads, 2 * HEAD_DIM),
            jnp.stack([r.reshape(bs, ss, d_sgu) for r in sgu], axis=0))
```

```python
import functools
import math

import jax
import jax.numpy as jnp
from jax import lax
from jax.experimental import pallas as pl
from jax.experimental.pallas import tpu as pltpu

F32 = jnp.float32
BF16 = jnp.bfloat16

CHUNK = 64
SGU_CHUNK = 128
SGU_GROUPS = 4
HEAD_DIM = 64
NUM_BUCKETS = 32
MAX_DISTANCE = 128
N_EXPERTS = 8
LN_EPS = 1e-5
NEG_INF = -1e30
SQRT_HALF = math.sqrt(0.5)
LOG2_E = math.log2(math.e)

LANES = 128
VMEM_LIMIT_BYTES = 56 * 1024 * 1024

SGU_ROWS = 256
FFN_ROWS = 512
FFN_SPLIT = 2
ROUTE_ROWS = 1024
MOVE_ROWS = 256
EXPERT_ROWS = 512
ATT_K = 256
CACHE_K = 512
HEAD_GROUP = 2


def _cparams(semantics):
    return pltpu.CompilerParams(dimension_semantics=semantics, vmem_limit_bytes=VMEM_LIMIT_BYTES)


def _resident(shape):
    nd = len(shape)
    return pl.BlockSpec(shape, lambda *_: (0,) * nd, pipeline_mode=pl.Buffered(1))


def _layer_norm(z, g, b):
    mu = jnp.mean(z, axis=-1, keepdims=True)
    zc = z - mu
    var = jnp.mean(zc * zc, axis=-1, keepdims=True)
    return zc * lax.rsqrt(var + LN_EPS) * g + b


def _gelu_exact(h):
    return 0.5 * h * (1.0 + lax.erf(h * SQRT_HALF))


def _sgu_kernel(x_ref, win_ref, lg_ref, lb_ref, ws_ref, bs_ref, wout_ref, mg_ref, mb_ref,
                o_ref, *rest, chunk, alpha, emit_v):
    if emit_v:
        v_ref, gate_ref = rest
    else:
        (gate_ref,) = rest
    rows = x_ref.shape[0]
    d_sgu = wout_ref.shape[0]
    gw = d_sgu // SGU_GROUPS
    x = x_ref[...]
    xb = x.astype(BF16)
    v = _gelu_exact(jnp.dot(xb, win_ref[:, d_sgu:], preferred_element_type=F32))
    vn = _layer_norm(v, lg_ref[...], lb_ref[...])
    if emit_v:
        v_ref[...] = vn
    vb = vn.astype(BF16)
    u = _gelu_exact(jnp.dot(xb, win_ref[:, :d_sgu], preferred_element_type=F32))
    tril = (lax.broadcasted_iota(jnp.int32, (chunk, chunk), 0)
            >= lax.broadcasted_iota(jnp.int32, (chunk, chunk), 1))
    for g in range(SGU_GROUPS):
        wg = jnp.where(tril, ws_ref[g], 0.0).astype(BF16)
        bcol = bs_ref[:, g:g + 1]
        for c in range(rows // chunk):
            r0, r1, c0, c1 = c * chunk, (c + 1) * chunk, g * gw, (g + 1) * gw
            mixed = jnp.dot(wg, vb[r0:r1, c0:c1], preferred_element_type=F32) + bcol
            gate_ref[r0:r1, c0:c1] = (u[r0:r1, c0:c1] * mixed).astype(BF16)
    y = jnp.dot(gate_ref[...], wout_ref[...], preferred_element_type=F32)
    o_ref[...] = _layer_norm(alpha * x + y, mg_ref[...], mb_ref[...])


def _sgu_layer(x, w_in, ln_g, ln_b, w_s, b_s, w_out, mix_g, mix_b, *, chunk, alpha, emit_v):
    n, d = x.shape
    d_sgu = w_out.shape[0]
    rows = min(SGU_ROWS, n)
    assert n % rows == 0 and rows % chunk == 0
    row_spec = pl.BlockSpec((rows, d), lambda i: (i, 0))
    out_shape = [jax.ShapeDtypeStruct((n, d), F32)]
    out_specs = [row_spec]
    if emit_v:
        out_shape.append(jax.ShapeDtypeStruct((n, d_sgu), F32))
        out_specs.append(pl.BlockSpec((rows, d_sgu), lambda i: (i, 0)))
    res = pl.pallas_call(
        functools.partial(_sgu_kernel, chunk=chunk, alpha=alpha, emit_v=emit_v),
        grid=(n // rows,),
        in_specs=[row_spec, _resident(w_in.shape), _resident(ln_g.shape), _resident(ln_b.shape),
                  _resident(w_s.shape), _resident(b_s.shape), _resident(w_out.shape),
                  _resident(mix_g.shape), _resident(mix_b.shape)],
        out_specs=out_specs,
        out_shape=out_shape,
        scratch_shapes=[pltpu.VMEM((rows, d_sgu), BF16)],
        name="sgu_layer",
        compiler_params=_cparams(("parallel",)),
    )(x, w_in, ln_g, ln_b, w_s, b_s, w_out, mix_g, mix_b)
    return res if emit_v else (res[0], None)


def _swiglu_piece(xb, wg, wu, wd):
    g = jnp.dot(xb, wg, preferred_element_type=F32)
    u = jnp.dot(xb, wu, preferred_element_type=F32)
    a = (g * jax.nn.sigmoid(g) * u).astype(BF16)
    return jnp.dot(a, wd, preferred_element_type=F32)


def _ffn_kernel(x_ref, wgu_ref, wd_ref, g_ref, b_ref, o_ref, *, alpha):
    d_ff = wd_ref.shape[0]
    piece = d_ff // FFN_SPLIT
    x = x_ref[...]
    xb = x.astype(BF16)
    acc = None
    for j in range(FFN_SPLIT):
        t = _swiglu_piece(xb, wgu_ref[:, j * piece:(j + 1) * piece],
                          wgu_ref[:, d_ff + j * piece:d_ff + (j + 1) * piece],
                          wd_ref[j * piece:(j + 1) * piece, :])
        acc = t if acc is None else acc + t
    o_ref[...] = _layer_norm(alpha * x + acc, g_ref[...], b_ref[...])


def _ffn_layer(x, w_gu, w_down, ln_g, ln_b, *, alpha):
    n, d = x.shape
    rows = min(FFN_ROWS, n)
    assert n % rows == 0 and w_down.shape[0] % (FFN_SPLIT * LANES) == 0
    row_spec = pl.BlockSpec((rows, d), lambda i: (i, 0))
    return pl.pallas_call(
        functools.partial(_ffn_kernel, alpha=alpha),
        grid=(n // rows,),
        in_specs=[row_spec, _resident(w_gu.shape), _resident(w_down.shape),
                  _resident(ln_g.shape), _resident(ln_b.shape)],
        out_specs=row_spec,
        out_shape=jax.ShapeDtypeStruct((n, d), F32),
        name="dense_ffn",
        compiler_params=_cparams(("parallel",)),
    )(x, w_gu, w_down, ln_g, ln_b)


def _router_kernel(x_ref, wr_ref, idx_ref, gate_ref):
    logits = lax.dot_general(wr_ref[...], x_ref[...], (((1,), (1,)), ((), ())),
                             precision=lax.Precision.HIGHEST, preferred_element_type=F32)
    ids = lax.broadcasted_iota(jnp.int32, logits.shape, 0)
    m1 = jnp.max(logits, axis=0, keepdims=True)
    i1 = jnp.min(jnp.where(logits == m1, ids, N_EXPERTS), axis=0, keepdims=True)
    rest = jnp.where(ids == i1, -jnp.inf, logits)
    m2 = jnp.max(rest, axis=0, keepdims=True)
    i2 = jnp.min(jnp.where(rest == m2, ids, N_EXPERTS), axis=0, keepdims=True)
    e2 = jnp.exp(m2 - m1)
    den = 1.0 + e2
    idx_ref[...] = jnp.concatenate([i1, i2], axis=0)
    gate_ref[...] = jnp.concatenate([1.0 / den, e2 / den], axis=0)


def _router(x, w_router_t):
    n, d = x.shape
    rows = min(ROUTE_ROWS, n)
    assert n % rows == 0
    return pl.pallas_call(
        _router_kernel,
        grid=(n // rows,),
        in_specs=[pl.BlockSpec((rows, d), lambda i: (i, 0)), _resident(w_router_t.shape)],
        out_specs=[pl.BlockSpec((2, rows), lambda i: (0, i)), pl.BlockSpec((2, rows), lambda i: (0, i))],
        out_shape=[jax.ShapeDtypeStruct((2, n), jnp.int32), jax.ShapeDtypeStruct((2, n), F32)],
        name="moe_router",
        compiler_params=_cparams(("parallel",)),
    )(x, w_router_t)


def _route_plan(idx, n_rows_padded):
    e_flat = idx.reshape(-1)
    onehot = (e_flat[:, None] == jnp.arange(N_EXPERTS, dtype=jnp.int32)[None, :]).astype(jnp.int32)
    csum = jnp.cumsum(onehot, axis=0)
    counts = csum[-1]
    padded = ((counts + EXPERT_ROWS - 1) // EXPERT_ROWS) * EXPERT_ROWS
    ends = jnp.cumsum(padded)
    starts = ends - padded
    dest = jnp.sum(onehot * (starts[None, :] + csum - 1), axis=1)
    tile_start = jnp.arange(n_rows_padded // EXPERT_ROWS, dtype=jnp.int32) * EXPERT_ROWS
    tile_expert = jnp.sum((tile_start[:, None] >= ends[None, :]).astype(jnp.int32), axis=1)
    return dest.astype(jnp.int32), jnp.minimum(tile_expert, N_EXPERTS - 1).astype(jnp.int32)


def _row_copy(src_ref, src_row, dst_ref, dst_row, sem):
    return pltpu.make_async_copy(src_ref.at[pl.ds(src_row, 1)], dst_ref.at[pl.ds(dst_row, 1)], sem)


def _dispatch_kernel(dest_ref, x_ref, xs_in_ref, xs_ref, sem):
    del xs_in_ref
    rows = x_ref.shape[0]

    def start(r, carry):
        _row_copy(x_ref, r, xs_ref, dest_ref[0, r], sem).start()
        _row_copy(x_ref, r, xs_ref, dest_ref[0, rows + r], sem).start()
        return carry

    def wait(r, carry):
        _row_copy(x_ref, 0, xs_ref, 0, sem).wait()
        _row_copy(x_ref, 0, xs_ref, 0, sem).wait()
        return carry

    lax.fori_loop(0, rows, start, 0)
    lax.fori_loop(0, rows, wait, 0)


def _dispatch(x, dest_tiles, n_rows_padded):
    n, d = x.shape
    rows = dest_tiles.shape[-1] // 2
    zeros = jnp.zeros((n_rows_padded, d), F32)
    return pl.pallas_call(
        _dispatch_kernel,
        grid=(n // rows,),
        in_specs=[pl.BlockSpec((None, 1, 2 * rows), lambda i: (i, 0, 0), memory_space=pltpu.SMEM),
                  pl.BlockSpec((rows, d), lambda i: (i, 0)),
                  pl.BlockSpec(memory_space=pl.ANY)],
        out_specs=pl.BlockSpec(memory_space=pl.ANY),
        out_shape=jax.ShapeDtypeStruct((n_rows_padded, d), F32),
        scratch_shapes=[pltpu.SemaphoreType.DMA(())],
        input_output_aliases={2: 0},
        name="moe_dispatch",
        compiler_params=_cparams(("arbitrary",)),
    )(dest_tiles, x, zeros)


def _expert_kernel(te_ref, xs_ref, wg_ref, wu_ref, wd_ref, o_ref):
    del te_ref
    j = pl.program_id(1)
    t = _swiglu_piece(xs_ref[...].astype(BF16), wg_ref[...], wu_ref[...], wd_ref[...])

    @pl.when(j == 0)
    def _():
        o_ref[...] = t

    @pl.when(j > 0)
    def _():
        o_ref[...] += t


def _experts(xs, tile_expert, w_gu, w_down):
    r, d = xs.shape
    d_ff = w_down.shape[1]
    piece = d_ff // FFN_SPLIT
    grid_spec = pltpu.PrefetchScalarGridSpec(
        num_scalar_prefetch=1,
        grid=(r // EXPERT_ROWS, FFN_SPLIT),
        in_specs=[pl.BlockSpec((EXPERT_ROWS, d), lambda t, j, te: (t, 0)),
                  pl.BlockSpec((None, d, piece), lambda t, j, te: (te[t], 0, j)),
                  pl.BlockSpec((None, d, piece), lambda t, j, te: (te[t], 0, FFN_SPLIT + j)),
                  pl.BlockSpec((None, piece, d), lambda t, j, te: (te[t], j, 0))],
        out_specs=pl.BlockSpec((EXPERT_ROWS, d), lambda t, j, te: (t, 0)),
    )
    return pl.pallas_call(
        _expert_kernel,
        grid_spec=grid_spec,
        out_shape=jax.ShapeDtypeStruct((r, d), F32),
        name="moe_experts",
        compiler_params=_cparams(("parallel", "arbitrary")),
    )(tile_expert, xs, w_gu, w_gu, w_down)


def _combine_kernel(dest_ref, x_ref, gate_ref, ys_ref, g_ref, b_ref, o_ref, buf_ref, sem, *, alpha):
    rows = x_ref.shape[0]

    def start(r, carry):
        _row_copy(ys_ref, dest_ref[0, r], buf_ref.at[0], r, sem).start()
        _row_copy(ys_ref, dest_ref[0, rows + r], buf_ref.at[1], r, sem).start()
        return carry

    def wait(r, carry):
        _row_copy(ys_ref, 0, buf_ref.at[0], 0, sem).wait()
        _row_copy(ys_ref, 0, buf_ref.at[1], 0, sem).wait()
        return carry

    lax.fori_loop(0, rows, start, 0)
    lax.fori_loop(0, rows, wait, 0)
    gates = gate_ref[...]
    y = gates[:, 0:1] * buf_ref[0] + gates[:, 1:2] * buf_ref[1]
    o_ref[...] = _layer_norm(alpha * x_ref[...] + y, g_ref[...], b_ref[...])


def _combine(x, dest_tiles, gates_t, ys, ln_g, ln_b, *, alpha):
    n, d = x.shape
    rows = dest_tiles.shape[-1] // 2
    row_spec = pl.BlockSpec((rows, d), lambda i: (i, 0))
    return pl.pallas_call(
        functools.partial(_combine_kernel, alpha=alpha),
        grid=(n // rows,),
        in_specs=[pl.BlockSpec((None, 1, 2 * rows), lambda i: (i, 0, 0), memory_space=pltpu.SMEM),
                  row_spec,
                  pl.BlockSpec((rows, 2), lambda i: (i, 0)),
                  pl.BlockSpec(memory_space=pl.ANY),
                  _resident(ln_g.shape), _resident(ln_b.shape)],
        out_specs=row_spec,
        out_shape=jax.ShapeDtypeStruct((n, d), F32),
        scratch_shapes=[pltpu.VMEM((2, rows, d), F32), pltpu.SemaphoreType.DMA(())],
        name="moe_combine",
        compiler_params=_cparams(("arbitrary",)),
    )(dest_tiles, x, gates_t, ys, ln_g, ln_b)


def _moe_layer(x, w_router_t, w_gu, w_down, ln_g, ln_b, *, alpha):
    n, _ = x.shape
    rows = min(MOVE_ROWS, n)
    assert n % rows == 0
    n_rows_padded = 2 * n + N_EXPERTS * EXPERT_ROWS
    idx, gates = _router(x, w_router_t)
    dest, tile_expert = _route_plan(idx, n_rows_padded)
    dest_tiles = dest.reshape(2, n // rows, rows).transpose(1, 0, 2).reshape(n // rows, 1, 2 * rows)
    xs = _dispatch(x, dest_tiles, n_rows_padded)
    ys = _experts(xs, tile_expert, w_gu, w_down)
    return _combine(x, dest_tiles, gates.T, ys, ln_g, ln_b, alpha=alpha)


def _dot_nt(a, b):
    return lax.dot_general(a, b, (((1,), (1,)), ((), ())), preferred_element_type=F32)


def _kv_kernel(x_ref, w_ref, wvt_ref, k_ref, v_ref, kb_ref, vt_ref):
    qk_w = k_ref.shape[-1]
    xb = x_ref[...].astype(BF16)
    kv = jnp.dot(xb, w_ref[...], preferred_element_type=F32)
    k = kv[:, :qk_w]
    k_ref[...] = k
    v_ref[...] = kv[:, qk_w:]
    kb_ref[...] = k.astype(BF16)
    vt_ref[...] = _dot_nt(wvt_ref[...], xb).astype(BF16)


def _kv_proj(x, w_kv, w_vt):
    bsz, t, d = x.shape
    qk_w = w_kv.shape[1] // 2
    rows = min(ATT_K, t)
    assert t % rows == 0
    row_spec = pl.BlockSpec((None, rows, qk_w), lambda b, i: (b, i, 0))
    return pl.pallas_call(
        _kv_kernel,
        grid=(bsz, t // rows),
        in_specs=[pl.BlockSpec((None, rows, d), lambda b, i: (b, i, 0)),
                  _resident(w_kv.shape), _resident(w_vt.shape)],
        out_specs=[row_spec, row_spec, row_spec,
                   pl.BlockSpec((None, None, qk_w, rows), lambda b, i: (b, i, 0, 0))],
        out_shape=[jax.ShapeDtypeStruct((bsz, t, qk_w), F32)] * 2
        + [jax.ShapeDtypeStruct((bsz, t, qk_w), BF16),
           jax.ShapeDtypeStruct((bsz, t // rows, qk_w, rows), BF16)],
        name="kv_proj",
        compiler_params=_cparams(("parallel", "parallel")),
    )(x, w_kv, w_vt)


def _t5_bucket(rel):
    nb = NUM_BUCKETS // 2
    max_exact = nb // 2
    ret = (rel > 0).astype(jnp.int32) * nb
    n = jnp.abs(rel)
    nf = jnp.maximum(n, 1).astype(jnp.float32)
    large = max_exact + (jnp.log(nf / max_exact) / math.log(MAX_DISTANCE / max_exact)
                         * (nb - max_exact)).astype(jnp.int32)
    large = jnp.minimum(large, nb - 1)
    return ret + jnp.where(n < max_exact, n, large)


def _bucket_table(q_pos, k_pos, visible=None):
    bucket = _t5_bucket(k_pos[:, None] - q_pos[None, :])
    return bucket if visible is None else jnp.where(visible, bucket, -1)


def _bias_kernel(far_ref, rb_ref, bucket_ref, o_ref):
    h = pl.program_id(0)
    bucket = bucket_ref[...]
    far = rb_ref[far_ref[0], h]
    out = jnp.zeros(bucket.shape, F32)
    for b in range(NUM_BUCKETS):
        out = jnp.where(bucket == b, rb_ref[b, h] - far, out)
    o_ref[...] = jnp.where(bucket < 0, NEG_INF, out * LOG2_E)


def _bias_tables(rel_bias, buckets):
    n_heads2 = rel_bias.shape[1]
    far_bucket = _t5_bucket(jnp.full((1,), -MAX_DISTANCE, jnp.int32))
    return pl.pallas_call(
        _bias_kernel,
        grid=(n_heads2,),
        in_specs=[pl.BlockSpec(memory_space=pltpu.SMEM), pl.BlockSpec(memory_space=pltpu.SMEM),
                  _resident(buckets.shape)],
        out_specs=pl.BlockSpec((None,) + buckets.shape, lambda h: (h, 0, 0)),
        out_shape=jax.ShapeDtypeStruct((n_heads2,) + buckets.shape, F32),
        name="bias_tables",
        compiler_params=_cparams(("parallel",)),
    )(far_bucket, rel_bias.astype(F32), buckets)


def _att_init(x_ref, wqt_ref, qa_ref, qb_ref, m_ref, l_ref, acc_ref):
    qt = _dot_nt(wqt_ref[...], x_ref[...].astype(BF16)) * (HEAD_DIM ** -0.5 * LOG2_E)
    row = lax.broadcasted_iota(jnp.int32, (LANES, qt.shape[1]), 0)
    for h in range(qa_ref.shape[0]):
        slab = qt[h * LANES:(h + 1) * LANES, :]
        qa_ref[h] = jnp.where(row < HEAD_DIM, slab, 0.0).astype(BF16)
        qb_ref[h] = jnp.where(row >= HEAD_DIM, slab, 0.0).astype(BF16)
    m_ref[...] = jnp.full(m_ref.shape, NEG_INF, F32)
    l_ref[...] = jnp.zeros(l_ref.shape, F32)
    acc_ref[...] = jnp.zeros(acc_ref.shape, F32)


def _att_block(qa_ref, qb_ref, k, vt_slab, bias, m_ref, l_ref, acc_ref):
    n_heads = 2 * qa_ref.shape[0]

    def scores(hh):
        h = hh // 2
        q_ref = qa_ref if hh % 2 == 0 else qb_ref
        s = jnp.dot(k[:, h * LANES:(h + 1) * LANES], q_ref[h], preferred_element_type=F32)
        return s if bias is None else s + bias(hh)

    s_of, p_of, scale_of = {}, {}, {}

    def softmax(hh):
        s = s_of.pop(hh)
        m_old = m_ref[hh]
        m_new = jnp.maximum(m_old, jnp.max(s, axis=0, keepdims=True))
        scale = jnp.exp2(m_old - m_new)
        p = jnp.exp2(s - m_new)
        l_ref[hh] = scale * l_ref[hh] + jnp.sum(p, axis=0, keepdims=True)
        m_ref[hh] = m_new
        p_of[hh] = p.astype(BF16)
        scale_of[hh] = scale

    def accumulate(hh):
        acc_ref[hh] = scale_of.pop(hh) * acc_ref[hh] + jnp.dot(vt_slab(hh // 2), p_of.pop(hh),
                                                               preferred_element_type=F32)

    n_groups = n_heads // HEAD_GROUP
    group = lambda t: range(t * HEAD_GROUP, (t + 1) * HEAD_GROUP)
    for t in range(-2, n_groups):
        if t + 2 < n_groups:
            for hh in group(t + 2):
                s_of[hh] = scores(hh)
        if 0 <= t + 1 < n_groups:
            for hh in group(t + 1):
                softmax(hh)
        if t >= 0:
            for hh in group(t):
                accumulate(hh)


def _att_finish(x_ref, lam_ref, sg_ref, wout_ref, g_ref, b_ref, o_ref, l_ref, acc_ref, heads_ref,
                *, lam_init, alpha):
    lp = lam_ref[...]
    lam = (jnp.exp(jnp.sum(lp[0:1] * lp[1:2], axis=-1, keepdims=True))
           - jnp.exp(jnp.sum(lp[2:3] * lp[3:4], axis=-1, keepdims=True)) + lam_init)
    for h in range(acc_ref.shape[0] // 2):
        a = acc_ref[2 * h] * (1.0 / l_ref[2 * h]) - (lam / l_ref[2 * h + 1]) * acc_ref[2 * h + 1]
        r = a * lax.rsqrt(jnp.mean(a * a, axis=0, keepdims=True) + LN_EPS) * sg_ref[...]
        heads_ref[h * LANES:(h + 1) * LANES, :] = r * (1.0 - lam_init)
    heads = jnp.transpose(heads_ref[...]).astype(BF16)
    y = jnp.dot(heads, wout_ref[...], preferred_element_type=F32)
    o_ref[...] = _layer_norm(alpha * x_ref[...] + y, g_ref[...], b_ref[...])


def _att_scratch(q_rows, width):
    n_slabs = width // LANES
    return [pltpu.VMEM((n_slabs, LANES, q_rows), BF16), pltpu.VMEM((n_slabs, LANES, q_rows), BF16),
            pltpu.VMEM((2 * n_slabs, 1, q_rows), F32), pltpu.VMEM((2 * n_slabs, 1, q_rows), F32),
            pltpu.VMEM((2 * n_slabs, LANES, q_rows), F32), pltpu.VMEM((width, q_rows), F32)]


def _prompt_att_kernel(qi_ref, kj_ref, x_ref, k_ref, vt_ref, wqt_ref, bias_ref, lam_ref, sg_ref,
                       wout_ref, g_ref, b_ref, o_ref, qa_ref, qb_ref, m_ref, l_ref, acc_ref, heads_ref,
                       *, lam_init, alpha):
    p = pl.program_id(1)
    qi = qi_ref[p]
    kj = kj_ref[p]
    tk = k_ref.shape[0]
    state = (m_ref, l_ref, acc_ref)

    def vt_slab(h):
        return vt_ref[h * LANES:(h + 1) * LANES, :]

    def diag_bias(hh):
        return bias_ref[hh, 0:tk, :]

    def sub_bias(hh):
        return bias_ref[hh, tk:2 * tk, :]

    @pl.when(kj == 0)
    def _():
        _att_init(x_ref, wqt_ref, qa_ref, qb_ref, *state)

    @pl.when(kj < qi - 1)
    def _():
        _att_block(qa_ref, qb_ref, k_ref[...], vt_slab, None, *state)

    @pl.when(kj == qi - 1)
    def _():
        _att_block(qa_ref, qb_ref, k_ref[...], vt_slab, sub_bias, *state)

    @pl.when(kj == qi)
    def _():
        _att_block(qa_ref, qb_ref, k_ref[...], vt_slab, diag_bias, *state)
        _att_finish(x_ref, lam_ref, sg_ref, wout_ref, g_ref, b_ref, o_ref, l_ref, acc_ref, heads_ref,
                    lam_init=lam_init, alpha=alpha)


def _prompt_bias(rel_bias, tq):
    pos = jnp.arange(tq, dtype=jnp.int32)
    visible = pos[:, None] < (pos[None, :] // CHUNK + 1) * CHUNK
    buckets = jnp.concatenate([_bucket_table(pos, pos, visible), _bucket_table(pos + tq, pos)], axis=0)
    return _bias_tables(rel_bias, buckets)


def _prompt_attention(x, kb, vt, bias, w_qt, lam_p, subln_g, w_out, ln_g, ln_b, *, lam_init, alpha):
    bsz, s_len, d = x.shape
    width = kb.shape[-1]
    tq = vt.shape[-1]
    assert s_len % tq == 0 and tq % CHUNK == 0 and tq >= MAX_DISTANCE
    n_q = s_len // tq
    pairs = [(i, j) for i in range(n_q) for j in range(i + 1)]
    qi = jnp.array([ij[0] for ij in pairs], jnp.int32)
    kj = jnp.array([ij[1] for ij in pairs], jnp.int32)
    grid_spec = pltpu.PrefetchScalarGridSpec(
        num_scalar_prefetch=2,
        grid=(bsz, len(pairs)),
        in_specs=[pl.BlockSpec((None, tq, d), lambda b, p, qi, kj: (b, qi[p], 0)),
                  pl.BlockSpec((None, tq, width), lambda b, p, qi, kj: (b, kj[p], 0)),
                  pl.BlockSpec((None, None, width, tq), lambda b, p, qi, kj: (b, kj[p], 0, 0)),
                  _resident(w_qt.shape), _resident(bias.shape),
                  _resident(lam_p.shape), _resident(subln_g.shape), _resident(w_out.shape),
                  _resident(ln_g.shape), _resident(ln_b.shape)],
        out_specs=pl.BlockSpec((None, tq, d), lambda b, p, qi, kj: (b, qi[p], 0)),
        scratch_shapes=_att_scratch(tq, width),
    )
    return pl.pallas_call(
        functools.partial(_prompt_att_kernel, lam_init=lam_init, alpha=alpha),
        grid_spec=grid_spec,
        out_shape=jax.ShapeDtypeStruct((bsz, s_len, d), F32),
        name="prompt_attention",
        compiler_params=_cparams(("parallel", "arbitrary")),
    )(qi, kj, x, kb, vt, w_qt, bias, lam_p, subln_g, w_out, ln_g, ln_b)


def _sample_att_kernel(x_ref, ck_ref, cv_ref, kn_ref, vtn_ref, wqt_ref, bias_ref, lam_ref, sg_ref,
                       wout_ref, g_ref, b_ref, o_ref, qa_ref, qb_ref, m_ref, l_ref, acc_ref, heads_ref,
                       *, lam_init, alpha):
    j = pl.program_id(1)
    n_cache = pl.num_programs(1) - 1
    tk = ck_ref.shape[0]
    t_new = kn_ref.shape[0]
    state = (m_ref, l_ref, acc_ref)

    def near_bias(hh):
        return bias_ref[hh, 0:tk, :]

    def new_bias(hh):
        return bias_ref[hh, tk:tk + t_new, :]

    def cached_vt_slab(h):
        return jnp.transpose(cv_ref[:, h * LANES:(h + 1) * LANES]).astype(BF16)

    def new_vt_slab(h):
        return vtn_ref[h * LANES:(h + 1) * LANES, :]

    @pl.when(j == 0)
    def _():
        _att_init(x_ref, wqt_ref, qa_ref, qb_ref, *state)

    @pl.when(j < n_cache - 1)
    def _():
        _att_block(qa_ref, qb_ref, ck_ref[...].astype(BF16), cached_vt_slab, None, *state)

    @pl.when(j == n_cache - 1)
    def _():
        _att_block(qa_ref, qb_ref, ck_ref[...].astype(BF16), cached_vt_slab, near_bias, *state)

    @pl.when(j == n_cache)
    def _():
        _att_block(qa_ref, qb_ref, kn_ref[...], new_vt_slab, new_bias, *state)
        _att_finish(x_ref, lam_ref, sg_ref, wout_ref, g_ref, b_ref, o_ref, l_ref, acc_ref, heads_ref,
                    lam_init=lam_init, alpha=alpha)


def _sample_bias(rel_bias, past, t):
    tk = min(CACHE_K, past)
    q_pos = past + jnp.arange(t, dtype=jnp.int32)
    buckets = jnp.concatenate([_bucket_table(q_pos, past - tk + jnp.arange(tk, dtype=jnp.int32)),
                               _bucket_table(q_pos, q_pos)], axis=0)
    return _bias_tables(rel_bias, buckets)


def _sample_attention(x, cache_k, cache_v, kb, vt, bias, w_qt, lam_p, subln_g, w_out, ln_g, ln_b,
                      *, lam_init, alpha):
    bsz, t, d = x.shape
    past = cache_k.shape[1]
    width = kb.shape[-1]
    tk = min(CACHE_K, past)
    assert past % tk == 0 and tk >= MAX_DISTANCE and vt.shape[1] == 1
    n_cache = past // tk
    cache_spec = pl.BlockSpec((None, tk, width), lambda b, j: (b, jnp.minimum(j, n_cache - 1), 0))
    x_spec = pl.BlockSpec((None, t, d), lambda b, j: (b, 0, 0))
    return pl.pallas_call(
        functools.partial(_sample_att_kernel, lam_init=lam_init, alpha=alpha),
        grid=(bsz, n_cache + 1),
        in_specs=[x_spec, cache_spec, cache_spec,
                  pl.BlockSpec((None, t, width), lambda b, j: (b, 0, 0)),
                  pl.BlockSpec((None, None, width, t), lambda b, j: (b, 0, 0, 0)),
                  _resident(w_qt.shape), _resident(bias.shape),
                  _resident(lam_p.shape), _resident(subln_g.shape), _resident(w_out.shape),
                  _resident(ln_g.shape), _resident(ln_b.shape)],
        out_specs=x_spec,
        out_shape=jax.ShapeDtypeStruct((bsz, t, d), F32),
        scratch_shapes=_att_scratch(t, width),
        name="sample_attention",
        compiler_params=_cparams(("parallel", "arbitrary")),
    )(x, cache_k, cache_v, kb, vt, w_qt, bias, lam_p, subln_g, w_out, ln_g, ln_b)


def _trunk(x, cache_k, cache_v, w, *, depth, emit_v):
    bsz, t, d = x.shape
    n = bsz * t
    n_a = depth // 2
    alpha = (2 * depth) ** 0.25
    chunk = min(t, SGU_CHUNK)
    h = x.reshape(n, d)
    sgu_rows = []
    k_new = v_new = kb = vt = None
    for i in range(depth):
        if i < n_a:
            h, v_rows = _sgu_layer(
                h, w["a_w_in"][i], w["a_ln_g"][i], w["a_ln_b"][i], w["a_w_s"][i][:, :chunk, :chunk],
                w["a_b_s"][i][:, :chunk].T, w["a_w_out"][i], w["ln_mix_g"][i], w["ln_mix_b"][i],
                chunk=chunk, alpha=alpha, emit_v=emit_v)
            sgu_rows.append(v_rows)
        else:
            if k_new is None:
                k_new, v_new, kb, vt = _kv_proj(h.reshape(bsz, t, d), w["w_kv"], w["w_vt"])
                if cache_k is None:
                    bias = _prompt_bias(w["rel_bias"], vt.shape[-1])
                else:
                    bias = _sample_bias(w["rel_bias"], cache_k.shape[1], t)
            j = i - n_a
            lam_init = 0.8 - 0.6 * math.exp(-0.3 * i)
            args = (kb, vt, bias, w["b_w_qt"][j], w["b_lam"][j], w["b_subln_g"][j], w["b_w_out"][j],
                    w["ln_mix_g"][i], w["ln_mix_b"][i])
            if cache_k is None:
                h3 = _prompt_attention(h.reshape(bsz, t, d), *args, lam_init=lam_init, alpha=alpha)
            else:
                past = cache_k.shape[1]
                h3 = _sample_attention(h.reshape(bsz, t, d), cache_k.reshape(bsz, past, -1),
                                       cache_v.reshape(bsz, past, -1), *args, lam_init=lam_init, alpha=alpha)
            h = h3.reshape(n, d)
        if i % 2 == 0:
            h = _ffn_layer(h, w["ffn_w_gu"][i // 2], w["ffn_w_down"][i // 2],
                           w["ln_ffn_g"][i], w["ln_ffn_b"][i], alpha=alpha)
        else:
            h = _moe_layer(h, w["moe_w_router_t"][i // 2], w["moe_w_gu"][i // 2], w["moe_w_down"][i // 2],
                           w["ln_ffn_g"][i], w["ln_ffn_b"][i], alpha=alpha)
    return h.reshape(bsz, t, d), k_new, v_new, sgu_rows


def kernel(x_prompt, x_sample, cache_k, cache_v, a_w_in, a_ln_g, a_ln_b, a_w_s, a_b_s, a_w_out, w_kv, b_w_q,
           b_lam, b_subln_g, b_w_out, rel_bias, ln_mix_g, ln_mix_b, ln_ffn_g, ln_ffn_b, ffn_w_gu, ffn_w_down,
           moe_w_router, moe_w_gu, moe_w_down):
    depth = ln_mix_g.shape[0]
    n_heads2 = rel_bias.shape[1]
    row = lambda a: a[:, None, :]
    w = dict(
        a_w_in=a_w_in.astype(BF16), a_ln_g=row(a_ln_g), a_ln_b=row(a_ln_b), a_w_s=a_w_s, a_b_s=a_b_s,
        a_w_out=a_w_out.astype(BF16), w_kv=w_kv.astype(BF16),
        w_vt=jnp.transpose(w_kv[:, w_kv.shape[1] // 2:]).astype(BF16),
        b_w_qt=jnp.swapaxes(b_w_q, 1, 2).astype(BF16), b_lam=b_lam,
        b_subln_g=b_subln_g[:, :, None], b_w_out=b_w_out.astype(BF16), rel_bias=rel_bias,
        ln_mix_g=row(ln_mix_g), ln_mix_b=row(ln_mix_b), ln_ffn_g=row(ln_ffn_g), ln_ffn_b=row(ln_ffn_b),
        ffn_w_gu=ffn_w_gu.astype(BF16), ffn_w_down=ffn_w_down.astype(BF16),
        moe_w_router_t=jnp.swapaxes(moe_w_router, 1, 2), moe_w_gu=moe_w_gu.astype(BF16),
        moe_w_down=moe_w_down.astype(BF16))
    y_p, k_p, v_p, _ = _trunk(x_prompt, None, None, w, depth=depth, emit_v=False)
    y_s, k_s, v_s, sgu = _trunk(x_sample, cache_k, cache_v, w, depth=depth, emit_v=True)
    bp, sp, _ = x_prompt.shape
    bs, ss, _ = x_sample.shape
    n_heads = n_heads2 // 2
    d_sgu = a_w_out.shape[1]
    return (y_p, y_s,
            k_p.reshape(bp, sp, n_heads2, HEAD_DIM), v_p.reshape(bp, sp, n_heads, 2 * HEAD_DIM),
            k_s.reshape(bs, ss, n_heads2, HEAD_DIM), v_s.reshape(bs, ss, n_heads, 2 * HEAD_DIM),
            jnp.stack([r.reshape(bs, ss, d_sgu) for r in sgu], axis=0))
```

```python
import functools
import math
from typing import NamedTuple

import jax
import jax.numpy as jnp
from jax import lax
from jax.experimental import pallas as pl
from jax.experimental.pallas import tpu as pltpu

F32 = jnp.float32
BF16 = jnp.bfloat16

CHUNK = 64
SGU_CHUNK = 128
SGU_GROUPS = 4
HEAD_DIM = 64
NUM_BUCKETS = 32
MAX_DISTANCE = 128
N_EXPERTS = 8
LN_EPS = 1e-5
NEG_INF = -1e30
SQRT_HALF = math.sqrt(0.5)
LOG2_E = math.log2(math.e)

LANES = 128
VMEM_LIMIT_BYTES = 56 * 1024 * 1024

SGU_ROWS = 256
FFN_ROWS = 512
FFN_SPLIT = 2
ROUTE_ROWS = 1024
MOVE_ROWS = 512
RUN_ROWS = 16
EXPERT_ROWS = 512
ATT_K = 256
CACHE_K = 512
HEAD_GROUP = 2


def _cparams(semantics):
    return pltpu.CompilerParams(dimension_semantics=semantics, vmem_limit_bytes=VMEM_LIMIT_BYTES)


class _Stacked(NamedTuple):
    array: jax.Array
    layer: int

    @property
    def shape(self):
        return self.array.shape[1:]


def _operand(a):
    return a.array if isinstance(a, _Stacked) else a


def _resident(a):
    nd = len(a.shape)
    if isinstance(a, _Stacked):
        return pl.BlockSpec((None,) + a.shape, lambda *_: (a.layer,) + (0,) * nd,
                            pipeline_mode=pl.Buffered(1))
    return pl.BlockSpec(a.shape, lambda *_: (0,) * nd, pipeline_mode=pl.Buffered(1))


def _layer_norm(z, g, b):
    mu = jnp.mean(z, axis=-1, keepdims=True)
    zc = z - mu
    var = jnp.mean(zc * zc, axis=-1, keepdims=True)
    return zc * lax.rsqrt(var + LN_EPS) * g + b


def _gelu_exact(h):
    return 0.5 * h * (1.0 + lax.erf(h * SQRT_HALF))


def _sgu_kernel(x_ref, win_ref, lg_ref, lb_ref, ws_ref, bs_ref, wout_ref, mg_ref, mb_ref,
                o_ref, *rest, chunk, alpha, emit_v):
    if emit_v:
        v_ref, gate_ref = rest
    else:
        (gate_ref,) = rest
    rows = x_ref.shape[0]
    d_sgu = wout_ref.shape[0]
    gw = d_sgu // SGU_GROUPS
    x = x_ref[...]
    xb = x.astype(BF16)
    v = _gelu_exact(jnp.dot(xb, win_ref[:, d_sgu:], preferred_element_type=F32))
    vn = _layer_norm(v, lg_ref[...], lb_ref[...])
    if emit_v:
        v_ref[...] = vn
    vb = vn.astype(BF16)
    u = _gelu_exact(jnp.dot(xb, win_ref[:, :d_sgu], preferred_element_type=F32))
    tril = (lax.broadcasted_iota(jnp.int32, (chunk, chunk), 0)
            >= lax.broadcasted_iota(jnp.int32, (chunk, chunk), 1))
    for g in range(SGU_GROUPS):
        wg = jnp.where(tril, ws_ref[g], 0.0).astype(BF16)
        bcol = bs_ref[:, g:g + 1]
        for c in range(rows // chunk):
            r0, r1, c0, c1 = c * chunk, (c + 1) * chunk, g * gw, (g + 1) * gw
            mixed = jnp.dot(wg, vb[r0:r1, c0:c1], preferred_element_type=F32) + bcol
            gate_ref[r0:r1, c0:c1] = (u[r0:r1, c0:c1] * mixed).astype(BF16)
    y = jnp.dot(gate_ref[...], wout_ref[...], preferred_element_type=F32)
    o_ref[...] = _layer_norm(alpha * x + y, mg_ref[...], mb_ref[...])


def _sgu_layer(x, w_in, ln_g, ln_b, w_s, b_s, w_out, mix_g, mix_b, *, chunk, alpha, emit_v):
    n, d = x.shape
    d_sgu = w_out.shape[0]
    rows = min(SGU_ROWS, n)
    assert n % rows == 0 and rows % chunk == 0
    row_spec = pl.BlockSpec((rows, d), lambda i: (i, 0))
    out_shape = [jax.ShapeDtypeStruct((n, d), F32)]
    out_specs = [row_spec]
    if emit_v:
        out_shape.append(jax.ShapeDtypeStruct((n, d_sgu), F32))
        out_specs.append(pl.BlockSpec((rows, d_sgu), lambda i: (i, 0)))
    res = pl.pallas_call(
        functools.partial(_sgu_kernel, chunk=chunk, alpha=alpha, emit_v=emit_v),
        grid=(n // rows,),
        in_specs=[row_spec] + [_resident(a) for a in (w_in, ln_g, ln_b, w_s, b_s, w_out, mix_g, mix_b)],
        out_specs=out_specs,
        out_shape=out_shape,
        scratch_shapes=[pltpu.VMEM((rows, d_sgu), BF16)],
        name="sgu_layer",
        compiler_params=_cparams(("parallel",)),
    )(x, _operand(w_in), ln_g, ln_b, w_s, b_s, _operand(w_out), mix_g, mix_b)
    return res if emit_v else (res[0], None)


def _swiglu_piece(xb, wg, wu, wd):
    g = jnp.dot(xb, wg, preferred_element_type=F32)
    u = jnp.dot(xb, wu, preferred_element_type=F32)
    a = (g * jax.nn.sigmoid(g) * u).astype(BF16)
    return jnp.dot(a, wd, preferred_element_type=F32)


def _ffn_kernel(x_ref, wgu_ref, wd_ref, g_ref, b_ref, o_ref, *, alpha):
    d_ff = wd_ref.shape[0]
    piece = d_ff // FFN_SPLIT
    x = x_ref[...]
    xb = x.astype(BF16)
    acc = None
    for j in range(FFN_SPLIT):
        t = _swiglu_piece(xb, wgu_ref[:, j * piece:(j + 1) * piece],
                          wgu_ref[:, d_ff + j * piece:d_ff + (j + 1) * piece],
                          wd_ref[j * piece:(j + 1) * piece, :])
        acc = t if acc is None else acc + t
    o_ref[...] = _layer_norm(alpha * x + acc, g_ref[...], b_ref[...])


def _ffn_layer(x, w_gu, w_down, ln_g, ln_b, *, alpha):
    n, d = x.shape
    rows = min(FFN_ROWS, n)
    assert n % rows == 0 and w_down.shape[0] % (FFN_SPLIT * LANES) == 0
    row_spec = pl.BlockSpec((rows, d), lambda i: (i, 0))
    return pl.pallas_call(
        functools.partial(_ffn_kernel, alpha=alpha),
        grid=(n // rows,),
        in_specs=[row_spec] + [_resident(a) for a in (w_gu, w_down, ln_g, ln_b)],
        out_specs=row_spec,
        out_shape=jax.ShapeDtypeStruct((n, d), F32),
        name="dense_ffn",
        compiler_params=_cparams(("parallel",)),
    )(x, _operand(w_gu), _operand(w_down), ln_g, ln_b)


def _router_kernel(x_ref, wr_ref, idx_ref, gate_ref):
    logits = lax.dot_general(wr_ref[...], x_ref[...], (((1,), (1,)), ((), ())),
                             precision=lax.Precision.HIGHEST, preferred_element_type=F32)
    ids = lax.broadcasted_iota(jnp.int32, logits.shape, 0)
    m1 = jnp.max(logits, axis=0, keepdims=True)
    i1 = jnp.min(jnp.where(logits == m1, ids, N_EXPERTS), axis=0, keepdims=True)
    rest = jnp.where(ids == i1, -jnp.inf, logits)
    m2 = jnp.max(rest, axis=0, keepdims=True)
    i2 = jnp.min(jnp.where(rest == m2, ids, N_EXPERTS), axis=0, keepdims=True)
    e2 = jnp.exp(m2 - m1)
    den = 1.0 + e2
    idx_ref[...] = jnp.concatenate([i1, i2], axis=0)
    gate_ref[...] = jnp.concatenate([1.0 / den, e2 / den], axis=0)


def _router(x, w_router_t):
    n, d = x.shape
    rows = min(ROUTE_ROWS, n)
    assert n % rows == 0
    return pl.pallas_call(
        _router_kernel,
        grid=(n // rows,),
        in_specs=[pl.BlockSpec((rows, d), lambda i: (i, 0)), _resident(w_router_t)],
        out_specs=[pl.BlockSpec((2, rows), lambda i: (0, i)), pl.BlockSpec((2, rows), lambda i: (0, i))],
        out_shape=[jax.ShapeDtypeStruct((2, n), jnp.int32), jax.ShapeDtypeStruct((2, n), F32)],
        name="moe_router",
        compiler_params=_cparams(("parallel",)),
    )(x, w_router_t)


class _RoutePlan(NamedTuple):
    local_dest: jax.Array
    runs: jax.Array
    tile_expert: jax.Array
    n_used: jax.Array


def _compact_rows(tile_rows):
    bound = 2 * tile_rows + N_EXPERTS * (RUN_ROWS - 1)
    return -(-bound // LANES) * LANES


def _sorted_rows(n_tokens, tile_rows):
    bound = 2 * n_tokens + (n_tokens // tile_rows) * N_EXPERTS * (RUN_ROWS - 1)
    return -(-bound // EXPERT_ROWS) * EXPERT_ROWS + N_EXPERTS * EXPERT_ROWS


def _route_plan(idx, tile_rows):
    n = idx.shape[1]
    n_tiles = n // tile_rows
    experts = idx.reshape(2, n_tiles, tile_rows).transpose(1, 0, 2).reshape(n_tiles, 2 * tile_rows)
    onehot = (experts[:, :, None] == jnp.arange(N_EXPERTS, dtype=jnp.int32)).astype(jnp.int32)
    csum = jnp.cumsum(onehot, axis=1)
    count = csum[:, -1, :]
    run = -(-count // RUN_ROWS) * RUN_ROWS
    local_off = jnp.cumsum(run, axis=1) - run
    before = jnp.cumsum(run, axis=0) - run
    group = -(-jnp.sum(run, axis=0) // EXPERT_ROWS) * EXPERT_ROWS
    ends = jnp.cumsum(group)
    sorted_off = (ends - group)[None, :] + before
    local_dest = jnp.sum(onehot * (local_off[:, None, :] + csum - 1), axis=2)
    runs = jnp.concatenate([local_off, sorted_off, run // RUN_ROWS], axis=1)[:, None, :]
    tile_start = jnp.arange(_sorted_rows(n, tile_rows) // EXPERT_ROWS, dtype=jnp.int32) * EXPERT_ROWS
    tile_expert = jnp.sum((tile_start[:, None] >= ends[None, :]).astype(jnp.int32), axis=1)
    return _RoutePlan(local_dest.astype(jnp.int32), runs.astype(jnp.int32),
                      jnp.minimum(tile_expert, N_EXPERTS - 1).astype(jnp.int32),
                      (ends[-1:] // EXPERT_ROWS).astype(jnp.int32))


def _move_runs(runs_ref, copy, sem_wait):
    total = 0
    for e in range(N_EXPERTS):
        local, dest, chunks = runs_ref[0, e], runs_ref[0, N_EXPERTS + e], runs_ref[0, 2 * N_EXPERTS + e]

        def start(j, carry, local=local, dest=dest):
            copy(pl.multiple_of(local + j * RUN_ROWS, RUN_ROWS),
                 pl.multiple_of(dest + j * RUN_ROWS, RUN_ROWS)).start()
            return carry

        lax.fori_loop(0, chunks, start, 0)
        total = total + chunks

    def wait(j, carry):
        sem_wait()
        return carry

    lax.fori_loop(0, total, wait, 0)


def _dispatch_kernel(runs_ref, dest_ref, x_ref, xs_in_ref, xs_ref, buf_ref, sem):
    del xs_in_ref
    row = lax.broadcasted_iota(jnp.int32, (buf_ref.shape[0], x_ref.shape[0]), 0)
    pick = jnp.where(row == dest_ref[0:1, :], 1.0, jnp.where(row == dest_ref[1:2, :], 1.0, 0.0))
    buf_ref[...] = jnp.dot(pick.astype(BF16), x_ref[...].astype(BF16),
                           preferred_element_type=F32).astype(BF16)

    def copy(local, dest):
        return pltpu.make_async_copy(buf_ref.at[pl.ds(local, RUN_ROWS)], xs_ref.at[pl.ds(dest, RUN_ROWS)], sem)

    _move_runs(runs_ref, copy, lambda: copy(0, 0).wait())


def _dispatch(x, plan, tile_rows):
    n, d = x.shape
    n_tiles = n // tile_rows
    sorted_rows = _sorted_rows(n, tile_rows)
    zeros = jnp.zeros((sorted_rows, d), BF16)
    return pl.pallas_call(
        _dispatch_kernel,
        grid=(n_tiles,),
        in_specs=[pl.BlockSpec((None, 1, 3 * N_EXPERTS), lambda i: (i, 0, 0), memory_space=pltpu.SMEM),
                  pl.BlockSpec((None, 2, tile_rows), lambda i: (i, 0, 0)),
                  pl.BlockSpec((tile_rows, d), lambda i: (i, 0)),
                  pl.BlockSpec(memory_space=pl.ANY)],
        out_specs=pl.BlockSpec(memory_space=pl.ANY),
        out_shape=jax.ShapeDtypeStruct((sorted_rows, d), BF16),
        scratch_shapes=[pltpu.VMEM((_compact_rows(tile_rows), d), BF16), pltpu.SemaphoreType.DMA(())],
        input_output_aliases={3: 0},
        name="moe_dispatch",
        compiler_params=_cparams(("arbitrary",)),
    )(plan.runs, plan.local_dest.reshape(n_tiles, 2, tile_rows), x, zeros)


def _expert_kernel(te_ref, used_ref, xs_ref, wg_ref, wu_ref, wd_ref, o_ref):
    del te_ref
    t = pl.program_id(0)
    j = pl.program_id(1)

    @pl.when(t < used_ref[0])
    def _():
        piece = _swiglu_piece(xs_ref[...], wg_ref[...], wu_ref[...], wd_ref[...])

        @pl.when(j == 0)
        def _():
            o_ref[...] = piece

        @pl.when(j > 0)
        def _():
            o_ref[...] += piece

    @pl.when(jnp.logical_and(t >= used_ref[0], j == 0))
    def _():
        o_ref[...] = jnp.zeros(o_ref.shape, F32)


def _experts(xs, plan, w_gu, w_down):
    r, d = xs.shape
    d_ff = w_down.shape[1]
    piece = d_ff // FFN_SPLIT
    layer = w_gu.layer
    grid_spec = pltpu.PrefetchScalarGridSpec(
        num_scalar_prefetch=2,
        grid=(r // EXPERT_ROWS, FFN_SPLIT),
        in_specs=[pl.BlockSpec((EXPERT_ROWS, d), lambda t, j, te, used: (t, 0)),
                  pl.BlockSpec((None, None, d, piece), lambda t, j, te, used: (layer, te[t], 0, j)),
                  pl.BlockSpec((None, None, d, piece), lambda t, j, te, used: (layer, te[t], 0, FFN_SPLIT + j)),
                  pl.BlockSpec((None, None, piece, d), lambda t, j, te, used: (layer, te[t], j, 0))],
        out_specs=pl.BlockSpec((EXPERT_ROWS, d), lambda t, j, te, used: (t, 0)),
    )
    return pl.pallas_call(
        _expert_kernel,
        grid_spec=grid_spec,
        out_shape=jax.ShapeDtypeStruct((r, d), F32),
        name="moe_experts",
        compiler_params=_cparams(("parallel", "arbitrary")),
    )(plan.tile_expert, plan.n_used, xs, w_gu.array, w_gu.array, w_down.array)


def _combine_kernel(runs_ref, x_ref, dest_ref, gate_ref, ys_ref, g_ref, b_ref, o_ref, buf_ref, sem, *, alpha):
    @pl.when(pl.program_id(0) == 0)
    def _():
        buf_ref[...] = jnp.zeros(buf_ref.shape, F32)

    def copy(local, src):
        return pltpu.make_async_copy(ys_ref.at[pl.ds(src, RUN_ROWS)], buf_ref.at[pl.ds(local, RUN_ROWS)], sem)

    _move_runs(runs_ref, copy, lambda: copy(0, 0).wait())
    y = buf_ref[...]
    hi = y.astype(BF16)
    lo = (y - hi.astype(F32)).astype(BF16)
    col = lax.broadcasted_iota(jnp.int32, (x_ref.shape[0], buf_ref.shape[0]), 1)
    gates = gate_ref[...]
    mixed = None
    for slot in range(2):
        pick = jnp.where(col == dest_ref[:, slot:slot + 1], 1.0, 0.0).astype(BF16)
        rows = (jnp.dot(pick, hi, preferred_element_type=F32) + jnp.dot(pick, lo, preferred_element_type=F32))
        term = gates[:, slot:slot + 1] * rows
        mixed = term if mixed is None else mixed + term
    o_ref[...] = _layer_norm(alpha * x_ref[...] + mixed, g_ref[...], b_ref[...])


def _combine(x, plan, gates_t, ys, ln_g, ln_b, *, tile_rows, alpha):
    n, d = x.shape
    n_tiles = n // tile_rows
    row_spec = pl.BlockSpec((tile_rows, d), lambda i: (i, 0))
    pair_spec = pl.BlockSpec((tile_rows, 2), lambda i: (i, 0))
    dest_cols = plan.local_dest.reshape(n_tiles, 2, tile_rows).transpose(0, 2, 1).reshape(n, 2)
    return pl.pallas_call(
        functools.partial(_combine_kernel, alpha=alpha),
        grid=(n_tiles,),
        in_specs=[pl.BlockSpec((None, 1, 3 * N_EXPERTS), lambda i: (i, 0, 0), memory_space=pltpu.SMEM),
                  row_spec, pair_spec, pair_spec,
                  pl.BlockSpec(memory_space=pl.ANY),
                  _resident(ln_g), _resident(ln_b)],
        out_specs=row_spec,
        out_shape=jax.ShapeDtypeStruct((n, d), F32),
        scratch_shapes=[pltpu.VMEM((_compact_rows(tile_rows), d), F32), pltpu.SemaphoreType.DMA(())],
        name="moe_combine",
        compiler_params=_cparams(("arbitrary",)),
    )(plan.runs, x, dest_cols, gates_t, ys, ln_g, ln_b)


def _moe_layer(x, w_router_t, w_gu, w_down, ln_g, ln_b, *, alpha):
    n, _ = x.shape
    tile_rows = min(MOVE_ROWS, n)
    assert n % tile_rows == 0
    idx, gates = _router(x, w_router_t)
    plan = _route_plan(idx, tile_rows)
    xs = _dispatch(x, plan, tile_rows)
    ys = _experts(xs, plan, w_gu, w_down)
    return _combine(x, plan, gates.T, ys, ln_g, ln_b, tile_rows=tile_rows, alpha=alpha)


def _dot_nt(a, b):
    return lax.dot_general(a, b, (((1,), (1,)), ((), ())), preferred_element_type=F32)


def _kv_kernel(x_ref, w_ref, wvt_ref, k_ref, v_ref, kb_ref, vt_ref):
    qk_w = k_ref.shape[-1]
    xb = x_ref[...].astype(BF16)
    kv = jnp.dot(xb, w_ref[...], preferred_element_type=F32)
    k = kv[:, :qk_w]
    k_ref[...] = k
    v_ref[...] = kv[:, qk_w:]
    kb_ref[...] = k.astype(BF16)
    vt_ref[...] = _dot_nt(wvt_ref[...], xb).astype(BF16)


def _kv_proj(x, w_kv, w_vt):
    bsz, t, d = x.shape
    qk_w = w_kv.shape[1] // 2
    rows = min(ATT_K, t)
    assert t % rows == 0
    row_spec = pl.BlockSpec((None, rows, qk_w), lambda b, i: (b, i, 0))
    return pl.pallas_call(
        _kv_kernel,
        grid=(bsz, t // rows),
        in_specs=[pl.BlockSpec((None, rows, d), lambda b, i: (b, i, 0)),
                  _resident(w_kv), _resident(w_vt)],
        out_specs=[row_spec, row_spec, row_spec,
                   pl.BlockSpec((None, None, qk_w, rows), lambda b, i: (b, i, 0, 0))],
        out_shape=[jax.ShapeDtypeStruct((bsz, t, qk_w), F32)] * 2
        + [jax.ShapeDtypeStruct((bsz, t, qk_w), BF16),
           jax.ShapeDtypeStruct((bsz, t // rows, qk_w, rows), BF16)],
        name="kv_proj",
        compiler_params=_cparams(("parallel", "parallel")),
    )(x, w_kv, w_vt)


def _t5_bucket(rel):
    nb = NUM_BUCKETS // 2
    max_exact = nb // 2
    ret = (rel > 0).astype(jnp.int32) * nb
    n = jnp.abs(rel)
    nf = jnp.maximum(n, 1).astype(jnp.float32)
    large = max_exact + (jnp.log(nf / max_exact) / math.log(MAX_DISTANCE / max_exact)
                         * (nb - max_exact)).astype(jnp.int32)
    large = jnp.minimum(large, nb - 1)
    return ret + jnp.where(n < max_exact, n, large)


def _bucket_table(q_pos, k_pos, visible=None):
    bucket = _t5_bucket(k_pos[:, None] - q_pos[None, :])
    return bucket if visible is None else jnp.where(visible, bucket, -1)


def _bias_kernel(far_ref, rb_ref, bucket_ref, o_ref):
    h = pl.program_id(0)
    bucket = bucket_ref[...]
    far = rb_ref[far_ref[0], h]
    out = jnp.zeros(bucket.shape, F32)
    for b in range(NUM_BUCKETS):
        out = jnp.where(bucket == b, rb_ref[b, h] - far, out)
    o_ref[...] = jnp.where(bucket < 0, NEG_INF, out * LOG2_E)


def _bias_tables(rel_bias, buckets):
    n_heads2 = rel_bias.shape[1]
    far_bucket = _t5_bucket(jnp.full((1,), -MAX_DISTANCE, jnp.int32))
    return pl.pallas_call(
        _bias_kernel,
        grid=(n_heads2,),
        in_specs=[pl.BlockSpec(memory_space=pltpu.SMEM), pl.BlockSpec(memory_space=pltpu.SMEM),
                  _resident(buckets)],
        out_specs=pl.BlockSpec((None,) + buckets.shape, lambda h: (h, 0, 0)),
        out_shape=jax.ShapeDtypeStruct((n_heads2,) + buckets.shape, F32),
        name="bias_tables",
        compiler_params=_cparams(("parallel",)),
    )(far_bucket, rel_bias.astype(F32), buckets)


def _att_init(x_ref, wqt_ref, qa_ref, qb_ref, m_ref, l_ref, acc_ref):
    qt = _dot_nt(wqt_ref[...], x_ref[...].astype(BF16)) * (HEAD_DIM ** -0.5 * LOG2_E)
    row = lax.broadcasted_iota(jnp.int32, (LANES, qt.shape[1]), 0)
    for h in range(qa_ref.shape[0]):
        slab = qt[h * LANES:(h + 1) * LANES, :]
        qa_ref[h] = jnp.where(row < HEAD_DIM, slab, 0.0).astype(BF16)
        qb_ref[h] = jnp.where(row >= HEAD_DIM, slab, 0.0).astype(BF16)
    m_ref[...] = jnp.full(m_ref.shape, NEG_INF, F32)
    l_ref[...] = jnp.zeros(l_ref.shape, F32)
    acc_ref[...] = jnp.zeros(acc_ref.shape, F32)


def _att_block(qa_ref, qb_ref, k, vt_slab, bias, m_ref, l_ref, acc_ref):
    n_heads = 2 * qa_ref.shape[0]

    def scores(hh):
        h = hh // 2
        q_ref = qa_ref if hh % 2 == 0 else qb_ref
        s = jnp.dot(k[:, h * LANES:(h + 1) * LANES], q_ref[h], preferred_element_type=F32)
        return s if bias is None else s + bias(hh)

    s_of, p_of, scale_of = {}, {}, {}

    def softmax(hh):
        s = s_of.pop(hh)
        m_old = m_ref[hh]
        m_new = jnp.maximum(m_old, jnp.max(s, axis=0, keepdims=True))
        scale = jnp.exp2(m_old - m_new)
        p = jnp.exp2(s - m_new)
        l_ref[hh] = scale * l_ref[hh] + jnp.sum(p, axis=0, keepdims=True)
        m_ref[hh] = m_new
        p_of[hh] = p.astype(BF16)
        scale_of[hh] = scale

    def accumulate(hh):
        acc_ref[hh] = scale_of.pop(hh) * acc_ref[hh] + jnp.dot(vt_slab(hh // 2), p_of.pop(hh),
                                                               preferred_element_type=F32)

    n_groups = n_heads // HEAD_GROUP
    group = lambda t: range(t * HEAD_GROUP, (t + 1) * HEAD_GROUP)
    for t in range(-2, n_groups):
        if t + 2 < n_groups:
            for hh in group(t + 2):
                s_of[hh] = scores(hh)
        if 0 <= t + 1 < n_groups:
            for hh in group(t + 1):
                softmax(hh)
        if t >= 0:
            for hh in group(t):
                accumulate(hh)


def _att_finish(x_ref, lam_ref, sg_ref, wout_ref, g_ref, b_ref, o_ref, l_ref, acc_ref, heads_ref,
                *, lam_init, alpha):
    lp = lam_ref[...]
    lam = (jnp.exp(jnp.sum(lp[0:1] * lp[1:2], axis=-1, keepdims=True))
           - jnp.exp(jnp.sum(lp[2:3] * lp[3:4], axis=-1, keepdims=True)) + lam_init)
    for h in range(acc_ref.shape[0] // 2):
        a = acc_ref[2 * h] * (1.0 / l_ref[2 * h]) - (lam / l_ref[2 * h + 1]) * acc_ref[2 * h + 1]
        r = a * lax.rsqrt(jnp.mean(a * a, axis=0, keepdims=True) + LN_EPS) * sg_ref[...]
        heads_ref[h * LANES:(h + 1) * LANES, :] = r * (1.0 - lam_init)
    heads = jnp.transpose(heads_ref[...]).astype(BF16)
    y = jnp.dot(heads, wout_ref[...], preferred_element_type=F32)
    o_ref[...] = _layer_norm(alpha * x_ref[...] + y, g_ref[...], b_ref[...])


def _att_scratch(q_rows, width):
    n_slabs = width // LANES
    return [pltpu.VMEM((n_slabs, LANES, q_rows), BF16), pltpu.VMEM((n_slabs, LANES, q_rows), BF16),
            pltpu.VMEM((2 * n_slabs, 1, q_rows), F32), pltpu.VMEM((2 * n_slabs, 1, q_rows), F32),
            pltpu.VMEM((2 * n_slabs, LANES, q_rows), F32), pltpu.VMEM((width, q_rows), F32)]


STEP_FAR_WIDE = 0
STEP_FAR_ONE = 1
STEP_NEAR = 2
FAR_TILES = 4


def _prompt_att_kernel(qi_ref, kind_ref, first_ref, wide_ref, ka_ref_idx, kb_ref_idx,
                       x_ref, kw_ref, vtw_ref, ka_ref, vta_ref, kb_ref, vtb_ref, wqt_ref, bias_ref, lam_ref,
                       sg_ref, wout_ref, g_ref, b_ref, o_ref, qa_ref, qb_ref, m_ref, l_ref, acc_ref,
                       heads_ref, *, lam_init, alpha):
    del wide_ref, ka_ref_idx, kb_ref_idx
    p = pl.program_id(1)
    kind = kind_ref[p]
    tk = ka_ref.shape[0]
    state = (m_ref, l_ref, acc_ref)

    def slab(ref, h):
        return ref[h * LANES:(h + 1) * LANES, :]

    @pl.when(first_ref[p] == 1)
    def _():
        _att_init(x_ref, wqt_ref, qa_ref, qb_ref, *state)

    @pl.when(kind == STEP_FAR_WIDE)
    def _():
        def vt_slab(h):
            return jnp.concatenate([vtw_ref[c, h * LANES:(h + 1) * LANES, :]
                                    for c in range(vtw_ref.shape[0])], axis=1)
        _att_block(qa_ref, qb_ref, kw_ref[...], vt_slab, None, *state)

    @pl.when(kind == STEP_FAR_ONE)
    def _():
        _att_block(qa_ref, qb_ref, ka_ref[...], lambda h: slab(vta_ref, h), None, *state)

    @pl.when(kind == STEP_NEAR)
    def _():
        hide = jnp.where(qi_ref[p] == 0, NEG_INF, 0.0)

        def bias(hh):
            return jnp.concatenate([bias_ref[hh, 0:tk, :] + hide, bias_ref[hh, tk:2 * tk, :]], axis=0)

        def vt_slab(h):
            return jnp.concatenate([slab(vta_ref, h), slab(vtb_ref, h)], axis=1)

        k = jnp.concatenate([ka_ref[...], kb_ref[...]], axis=0)
        _att_block(qa_ref, qb_ref, k, vt_slab, bias, *state)
        _att_finish(x_ref, lam_ref, sg_ref, wout_ref, g_ref, b_ref, o_ref, l_ref, acc_ref, heads_ref,
                    lam_init=lam_init, alpha=alpha)


def _prompt_bias(rel_bias, tq):
    pos = jnp.arange(tq, dtype=jnp.int32)
    visible = pos[:, None] < (pos[None, :] // CHUNK + 1) * CHUNK
    buckets = jnp.concatenate([_bucket_table(pos + tq, pos), _bucket_table(pos, pos, visible)], axis=0)
    return _bias_tables(rel_bias, buckets)


def _prompt_steps(n_q):
    steps = []
    wide = ka = kb = 0
    for i in range(n_q):
        n_far = max(i - 1, 0)
        first = 1
        for c in range(n_far // FAR_TILES):
            wide = c
            steps.append((i, STEP_FAR_WIDE, first, wide, ka, kb))
            first = 0
        for j in range(n_far - n_far % FAR_TILES, n_far):
            ka = j
            steps.append((i, STEP_FAR_ONE, first, wide, ka, kb))
            first = 0
        ka, kb = max(i - 1, 0), i
        steps.append((i, STEP_NEAR, first, wide, ka, kb))
    return [jnp.array(col, jnp.int32) for col in zip(*steps)]


def _prompt_attention(x, kb, vt, bias, w_qt, lam_p, subln_g, w_out, ln_g, ln_b, *, lam_init, alpha):
    bsz, s_len, d = x.shape
    width = kb.shape[-1]
    tq = vt.shape[-1]
    n_q = s_len // tq
    assert s_len % tq == 0 and tq % CHUNK == 0 and tq >= MAX_DISTANCE
    assert n_q % FAR_TILES == 0 or n_q <= FAR_TILES
    steps = _prompt_steps(n_q)
    wide_tiles = min(FAR_TILES, n_q)
    x_spec = pl.BlockSpec((None, tq, d), lambda b, p, qi, kind, first, wide, ka, kb: (b, qi[p], 0))
    grid_spec = pltpu.PrefetchScalarGridSpec(
        num_scalar_prefetch=len(steps),
        grid=(bsz, steps[0].shape[0]),
        in_specs=[x_spec,
                  pl.BlockSpec((None, wide_tiles * tq, width),
                               lambda b, p, qi, kind, first, wide, ka, kb: (b, wide[p], 0)),
                  pl.BlockSpec((None, wide_tiles, width, tq),
                               lambda b, p, qi, kind, first, wide, ka, kb: (b, wide[p], 0, 0)),
                  pl.BlockSpec((None, tq, width), lambda b, p, qi, kind, first, wide, ka, kb: (b, ka[p], 0)),
                  pl.BlockSpec((None, None, width, tq),
                               lambda b, p, qi, kind, first, wide, ka, kb: (b, ka[p], 0, 0)),
                  pl.BlockSpec((None, tq, width), lambda b, p, qi, kind, first, wide, ka, kb: (b, kb[p], 0)),
                  pl.BlockSpec((None, None, width, tq),
                               lambda b, p, qi, kind, first, wide, ka, kb: (b, kb[p], 0, 0)),
                  _resident(w_qt), _resident(bias), _resident(lam_p), _resident(subln_g), _resident(w_out),
                  _resident(ln_g), _resident(ln_b)],
        out_specs=x_spec,
        scratch_shapes=_att_scratch(tq, width),
    )
    return pl.pallas_call(
        functools.partial(_prompt_att_kernel, lam_init=lam_init, alpha=alpha),
        grid_spec=grid_spec,
        out_shape=jax.ShapeDtypeStruct((bsz, s_len, d), F32),
        name="prompt_attention",
        compiler_params=_cparams(("parallel", "arbitrary")),
    )(*steps, x, kb, vt, kb, vt, kb, vt, _operand(w_qt), bias, lam_p, subln_g, _operand(w_out), ln_g, ln_b)


def _sample_att_kernel(x_ref, ck_ref, cv_ref, kn_ref, vtn_ref, wqt_ref, bias_ref, lam_ref, sg_ref,
                       wout_ref, g_ref, b_ref, o_ref, qa_ref, qb_ref, m_ref, l_ref, acc_ref, heads_ref,
                       *, lam_init, alpha):
    j = pl.program_id(1)
    n_cache = pl.num_programs(1) - 1
    tk = ck_ref.shape[0]
    t_new = kn_ref.shape[0]
    state = (m_ref, l_ref, acc_ref)

    def near_bias(hh):
        return bias_ref[hh, 0:tk, :]

    def new_bias(hh):
        return bias_ref[hh, tk:tk + t_new, :]

    def cached_vt_slab(h):
        return jnp.transpose(cv_ref[:, h * LANES:(h + 1) * LANES]).astype(BF16)

    def new_vt_slab(h):
        return vtn_ref[h * LANES:(h + 1) * LANES, :]

    @pl.when(j == 0)
    def _():
        _att_init(x_ref, wqt_ref, qa_ref, qb_ref, *state)

    @pl.when(j < n_cache - 1)
    def _():
        _att_block(qa_ref, qb_ref, ck_ref[...].astype(BF16), cached_vt_slab, None, *state)

    @pl.when(j == n_cache - 1)
    def _():
        _att_block(qa_ref, qb_ref, ck_ref[...].astype(BF16), cached_vt_slab, near_bias, *state)

    @pl.when(j == n_cache)
    def _():
        _att_block(qa_ref, qb_ref, kn_ref[...], new_vt_slab, new_bias, *state)
        _att_finish(x_ref, lam_ref, sg_ref, wout_ref, g_ref, b_ref, o_ref, l_ref, acc_ref, heads_ref,
                    lam_init=lam_init, alpha=alpha)


def _sample_bias(rel_bias, past, t):
    tk = min(CACHE_K, past)
    q_pos = past + jnp.arange(t, dtype=jnp.int32)
    buckets = jnp.concatenate([_bucket_table(q_pos, past - tk + jnp.arange(tk, dtype=jnp.int32)),
                               _bucket_table(q_pos, q_pos)], axis=0)
    return _bias_tables(rel_bias, buckets)


def _sample_attention(x, cache_k, cache_v, kb, vt, bias, w_qt, lam_p, subln_g, w_out, ln_g, ln_b,
                      *, lam_init, alpha):
    bsz, t, d = x.shape
    past = cache_k.shape[1]
    width = kb.shape[-1]
    tk = min(CACHE_K, past)
    assert past % tk == 0 and tk >= MAX_DISTANCE and vt.shape[1] == 1
    n_cache = past // tk
    cache_spec = pl.BlockSpec((None, tk, width), lambda b, j: (b, jnp.minimum(j, n_cache - 1), 0))
    x_spec = pl.BlockSpec((None, t, d), lambda b, j: (b, 0, 0))
    return pl.pallas_call(
        functools.partial(_sample_att_kernel, lam_init=lam_init, alpha=alpha),
        grid=(bsz, n_cache + 1),
        in_specs=[x_spec, cache_spec, cache_spec,
                  pl.BlockSpec((None, t, width), lambda b, j: (b, 0, 0)),
                  pl.BlockSpec((None, None, width, t), lambda b, j: (b, 0, 0, 0)),
                  _resident(w_qt), _resident(bias), _resident(lam_p), _resident(subln_g), _resident(w_out),
                  _resident(ln_g), _resident(ln_b)],
        out_specs=x_spec,
        out_shape=jax.ShapeDtypeStruct((bsz, t, d), F32),
        scratch_shapes=_att_scratch(t, width),
        name="sample_attention",
        compiler_params=_cparams(("parallel", "arbitrary")),
    )(x, cache_k, cache_v, kb, vt, _operand(w_qt), bias, lam_p, subln_g, _operand(w_out), ln_g, ln_b)


def _trunk(x, cache_k, cache_v, w, *, depth, emit_v):
    bsz, t, d = x.shape
    n = bsz * t
    n_a = depth // 2
    alpha = (2 * depth) ** 0.25
    chunk = min(t, SGU_CHUNK)
    h = x.reshape(n, d)
    sgu_rows = []
    k_new = v_new = kb = vt = None
    for i in range(depth):
        if i < n_a:
            h, v_rows = _sgu_layer(
                h, _Stacked(w["a_w_in"], i), w["a_ln_g"][i], w["a_ln_b"][i], w["a_w_s"][i][:, :chunk, :chunk],
                w["a_b_s"][i][:, :chunk].T, _Stacked(w["a_w_out"], i), w["ln_mix_g"][i], w["ln_mix_b"][i],
                chunk=chunk, alpha=alpha, emit_v=emit_v)
            sgu_rows.append(v_rows)
        else:
            if k_new is None:
                k_new, v_new, kb, vt = _kv_proj(h.reshape(bsz, t, d), w["w_kv"], w["w_vt"])
                if cache_k is None:
                    bias = _prompt_bias(w["rel_bias"], vt.shape[-1])
                else:
                    bias = _sample_bias(w["rel_bias"], cache_k.shape[1], t)
            j = i - n_a
            lam_init = 0.8 - 0.6 * math.exp(-0.3 * i)
            args = (kb, vt, bias, _Stacked(w["b_w_qt"], j), w["b_lam"][j], w["b_subln_g"][j],
                    _Stacked(w["b_w_out"], j), w["ln_mix_g"][i], w["ln_mix_b"][i])
            if cache_k is None:
                h3 = _prompt_attention(h.reshape(bsz, t, d), *args, lam_init=lam_init, alpha=alpha)
            else:
                past = cache_k.shape[1]
                h3 = _sample_attention(h.reshape(bsz, t, d), cache_k.reshape(bsz, past, -1),
                                       cache_v.reshape(bsz, past, -1), *args, lam_init=lam_init, alpha=alpha)
            h = h3.reshape(n, d)
        if i % 2 == 0:
            h = _ffn_layer(h, _Stacked(w["ffn_w_gu"], i // 2), _Stacked(w["ffn_w_down"], i // 2),
                           w["ln_ffn_g"][i], w["ln_ffn_b"][i], alpha=alpha)
        else:
            h = _moe_layer(h, w["moe_w_router_t"][i // 2], _Stacked(w["moe_w_gu"], i // 2),
                           _Stacked(w["moe_w_down"], i // 2), w["ln_ffn_g"][i], w["ln_ffn_b"][i], alpha=alpha)
    return h.reshape(bsz, t, d), k_new, v_new, sgu_rows


def kernel(x_prompt, x_sample, cache_k, cache_v, a_w_in, a_ln_g, a_ln_b, a_w_s, a_b_s, a_w_out, w_kv, b_w_q,
           b_lam, b_subln_g, b_w_out, rel_bias, ln_mix_g, ln_mix_b, ln_ffn_g, ln_ffn_b, ffn_w_gu, ffn_w_down,
           moe_w_router, moe_w_gu, moe_w_down):
    depth = ln_mix_g.shape[0]
    n_heads2 = rel_bias.shape[1]
    row = lambda a: a[:, None, :]
    w = dict(
        a_w_in=a_w_in.astype(BF16), a_ln_g=row(a_ln_g), a_ln_b=row(a_ln_b), a_w_s=a_w_s, a_b_s=a_b_s,
        a_w_out=a_w_out.astype(BF16), w_kv=w_kv.astype(BF16),
        w_vt=jnp.transpose(w_kv[:, w_kv.shape[1] // 2:]).astype(BF16),
        b_w_qt=jnp.swapaxes(b_w_q, 1, 2).astype(BF16), b_lam=b_lam,
        b_subln_g=b_subln_g[:, :, None], b_w_out=b_w_out.astype(BF16), rel_bias=rel_bias,
        ln_mix_g=row(ln_mix_g), ln_mix_b=row(ln_mix_b), ln_ffn_g=row(ln_ffn_g), ln_ffn_b=row(ln_ffn_b),
        ffn_w_gu=ffn_w_gu.astype(BF16), ffn_w_down=ffn_w_down.astype(BF16),
        moe_w_router_t=jnp.swapaxes(moe_w_router, 1, 2), moe_w_gu=moe_w_gu.astype(BF16),
        moe_w_down=moe_w_down.astype(BF16))
    y_p, k_p, v_p, _ = _trunk(x_prompt, None, None, w, depth=depth, emit_v=False)
    y_s, k_s, v_s, sgu = _trunk(x_sample, cache_k, cache_v, w, depth=depth, emit_v=True)
    bp, sp, _ = x_prompt.shape
    bs, ss, _ = x_sample.shape
    n_heads = n_heads2 // 2
    d_sgu = a_w_out.shape[1]
    return (y_p, y_s,
            k_p.reshape(bp, sp, n_heads2, HEAD_DIM), v_p.reshape(bp, sp, n_heads, 2 * HEAD_DIM),
            k_s.reshape(bs, ss, n_heads2, HEAD_DIM), v_s.reshape(bs, ss, n_heads, 2 * HEAD_DIM),
            jnp.stack([r.reshape(bs, ss, d_sgu) for r in sgu], axis=0))
```

```python
import functools
import math
from typing import NamedTuple

import jax
import jax.numpy as jnp
from jax import lax
from jax.experimental import pallas as pl
from jax.experimental.pallas import tpu as pltpu

F32 = jnp.float32
BF16 = jnp.bfloat16

CHUNK = 64
SGU_CHUNK = 128
SGU_GROUPS = 4
HEAD_DIM = 64
NUM_BUCKETS = 32
MAX_DISTANCE = 128
N_EXPERTS = 8
LN_EPS = 1e-5
NEG_INF = -1e30
SQRT_HALF = math.sqrt(0.5)
LOG2_E = math.log2(math.e)

LANES = 128
VMEM_LIMIT_BYTES = 56 * 1024 * 1024

SGU_ROWS = 256
FFN_ROWS = 512
FFN_SPLIT = 2
ROUTE_ROWS = 1024
MOVE_ROWS = 512
RUN_ROWS = 16
EXPERT_ROWS = 512
ATT_K = 256
CACHE_K = 512
HEAD_GROUP = 2


def _cparams(semantics):
    return pltpu.CompilerParams(dimension_semantics=semantics, vmem_limit_bytes=VMEM_LIMIT_BYTES)


class _Stacked(NamedTuple):
    array: jax.Array
    layer: int

    @property
    def shape(self):
        return self.array.shape[1:]


def _operand(a):
    return a.array if isinstance(a, _Stacked) else a


def _resident(a):
    nd = len(a.shape)
    if isinstance(a, _Stacked):
        return pl.BlockSpec((None,) + a.shape, lambda *_: (a.layer,) + (0,) * nd,
                            pipeline_mode=pl.Buffered(1))
    return pl.BlockSpec(a.shape, lambda *_: (0,) * nd, pipeline_mode=pl.Buffered(1))


def _layer_norm(z, g, b):
    mu = jnp.mean(z, axis=-1, keepdims=True)
    zc = z - mu
    var = jnp.mean(zc * zc, axis=-1, keepdims=True)
    return zc * lax.rsqrt(var + LN_EPS) * g + b


def _gelu_exact(h):
    return 0.5 * h * (1.0 + lax.erf(h * SQRT_HALF))


def _sgu_kernel(x_ref, win_ref, lg_ref, lb_ref, ws_ref, bs_ref, wout_ref, mg_ref, mb_ref,
                o_ref, *rest, chunk, alpha, emit_v):
    if emit_v:
        v_ref, gate_ref = rest
    else:
        (gate_ref,) = rest
    rows = x_ref.shape[0]
    d_sgu = wout_ref.shape[0]
    gw = d_sgu // SGU_GROUPS
    x = x_ref[...]
    xb = x.astype(BF16)
    v = _gelu_exact(jnp.dot(xb, win_ref[:, d_sgu:], preferred_element_type=F32))
    vn = _layer_norm(v, lg_ref[...], lb_ref[...])
    if emit_v:
        v_ref[...] = vn
    vb = vn.astype(BF16)
    u = _gelu_exact(jnp.dot(xb, win_ref[:, :d_sgu], preferred_element_type=F32))
    tril = (lax.broadcasted_iota(jnp.int32, (chunk, chunk), 0)
            >= lax.broadcasted_iota(jnp.int32, (chunk, chunk), 1))
    for g in range(SGU_GROUPS):
        wg = jnp.where(tril, ws_ref[g], 0.0).astype(BF16)
        bcol = bs_ref[:, g:g + 1]
        for c in range(rows // chunk):
            r0, r1, c0, c1 = c * chunk, (c + 1) * chunk, g * gw, (g + 1) * gw
            mixed = jnp.dot(wg, vb[r0:r1, c0:c1], preferred_element_type=F32) + bcol
            gate_ref[r0:r1, c0:c1] = (u[r0:r1, c0:c1] * mixed).astype(BF16)
    y = jnp.dot(gate_ref[...], wout_ref[...], preferred_element_type=F32)
    o_ref[...] = _layer_norm(alpha * x + y, mg_ref[...], mb_ref[...])


def _sgu_layer(x, w_in, ln_g, ln_b, w_s, b_s, w_out, mix_g, mix_b, *, chunk, alpha, emit_v):
    n, d = x.shape
    d_sgu = w_out.shape[0]
    rows = min(SGU_ROWS, n)
    assert n % rows == 0 and rows % chunk == 0
    row_spec = pl.BlockSpec((rows, d), lambda i: (i, 0))
    out_shape = [jax.ShapeDtypeStruct((n, d), F32)]
    out_specs = [row_spec]
    if emit_v:
        out_shape.append(jax.ShapeDtypeStruct((n, d_sgu), F32))
        out_specs.append(pl.BlockSpec((rows, d_sgu), lambda i: (i, 0)))
    res = pl.pallas_call(
        functools.partial(_sgu_kernel, chunk=chunk, alpha=alpha, emit_v=emit_v),
        grid=(n // rows,),
        in_specs=[row_spec] + [_resident(a) for a in (w_in, ln_g, ln_b, w_s, b_s, w_out, mix_g, mix_b)],
        out_specs=out_specs,
        out_shape=out_shape,
        scratch_shapes=[pltpu.VMEM((rows, d_sgu), BF16)],
        name="sgu_layer",
        compiler_params=_cparams(("parallel",)),
    )(x, _operand(w_in), ln_g, ln_b, w_s, b_s, _operand(w_out), mix_g, mix_b)
    return res if emit_v else (res[0], None)


def _swiglu_piece(xb, wg, wu, wd):
    g = jnp.dot(xb, wg, preferred_element_type=F32)
    u = jnp.dot(xb, wu, preferred_element_type=F32)
    a = (g * jax.nn.sigmoid(g) * u).astype(BF16)
    return jnp.dot(a, wd, preferred_element_type=F32)


def _ffn_kernel(x_ref, wgu_ref, wd_ref, g_ref, b_ref, o_ref, *, alpha):
    d_ff = wd_ref.shape[0]
    piece = d_ff // FFN_SPLIT
    x = x_ref[...]
    xb = x.astype(BF16)
    acc = None
    for j in range(FFN_SPLIT):
        t = _swiglu_piece(xb, wgu_ref[:, j * piece:(j + 1) * piece],
                          wgu_ref[:, d_ff + j * piece:d_ff + (j + 1) * piece],
                          wd_ref[j * piece:(j + 1) * piece, :])
        acc = t if acc is None else acc + t
    o_ref[...] = _layer_norm(alpha * x + acc, g_ref[...], b_ref[...])


def _ffn_layer(x, w_gu, w_down, ln_g, ln_b, *, alpha):
    n, d = x.shape
    rows = min(FFN_ROWS, n)
    assert n % rows == 0 and w_down.shape[0] % (FFN_SPLIT * LANES) == 0
    row_spec = pl.BlockSpec((rows, d), lambda i: (i, 0))
    return pl.pallas_call(
        functools.partial(_ffn_kernel, alpha=alpha),
        grid=(n // rows,),
        in_specs=[row_spec] + [_resident(a) for a in (w_gu, w_down, ln_g, ln_b)],
        out_specs=row_spec,
        out_shape=jax.ShapeDtypeStruct((n, d), F32),
        name="dense_ffn",
        compiler_params=_cparams(("parallel",)),
    )(x, _operand(w_gu), _operand(w_down), ln_g, ln_b)


def _router_kernel(x_ref, wr_ref, idx_ref, gate_ref):
    logits = lax.dot_general(wr_ref[...], x_ref[...], (((1,), (1,)), ((), ())),
                             precision=lax.Precision.HIGHEST, preferred_element_type=F32)
    ids = lax.broadcasted_iota(jnp.int32, logits.shape, 0)
    m1 = jnp.max(logits, axis=0, keepdims=True)
    i1 = jnp.min(jnp.where(logits == m1, ids, N_EXPERTS), axis=0, keepdims=True)
    rest = jnp.where(ids == i1, -jnp.inf, logits)
    m2 = jnp.max(rest, axis=0, keepdims=True)
    i2 = jnp.min(jnp.where(rest == m2, ids, N_EXPERTS), axis=0, keepdims=True)
    e2 = jnp.exp(m2 - m1)
    den = 1.0 + e2
    idx_ref[...] = jnp.concatenate([i1, i2], axis=0)
    gate_ref[...] = jnp.concatenate([1.0 / den, e2 / den], axis=0)


def _router(x, w_router_t):
    n, d = x.shape
    rows = min(ROUTE_ROWS, n)
    assert n % rows == 0
    return pl.pallas_call(
        _router_kernel,
        grid=(n // rows,),
        in_specs=[pl.BlockSpec((rows, d), lambda i: (i, 0)), _resident(w_router_t)],
        out_specs=[pl.BlockSpec((2, rows), lambda i: (0, i)), pl.BlockSpec((2, rows), lambda i: (0, i))],
        out_shape=[jax.ShapeDtypeStruct((2, n), jnp.int32), jax.ShapeDtypeStruct((2, n), F32)],
        name="moe_router",
        compiler_params=_cparams(("parallel",)),
    )(x, w_router_t)


class _RoutePlan(NamedTuple):
    local_dest: jax.Array
    runs: jax.Array
    tile_expert: jax.Array
    n_used: jax.Array


def _compact_rows(tile_rows):
    bound = 2 * tile_rows + N_EXPERTS * (RUN_ROWS - 1)
    return -(-bound // LANES) * LANES


def _sorted_rows(n_tokens, tile_rows):
    bound = 2 * n_tokens + (n_tokens // tile_rows) * N_EXPERTS * (RUN_ROWS - 1)
    return -(-bound // EXPERT_ROWS) * EXPERT_ROWS + N_EXPERTS * EXPERT_ROWS


def _route_plan(idx, tile_rows):
    n = idx.shape[1]
    n_tiles = n // tile_rows
    experts = idx.reshape(2, n_tiles, tile_rows).transpose(1, 0, 2).reshape(n_tiles, 2 * tile_rows)
    onehot = (experts[:, :, None] == jnp.arange(N_EXPERTS, dtype=jnp.int32)).astype(jnp.int32)
    order = jnp.arange(2 * tile_rows, dtype=jnp.int32)
    tri = (order[:, None] >= order[None, :]).astype(BF16)
    csum = jnp.einsum("uv,ive->iue", tri, onehot.astype(BF16),
                      preferred_element_type=F32).astype(jnp.int32)
    count = csum[:, -1, :]
    run = -(-count // RUN_ROWS) * RUN_ROWS
    local_off = jnp.cumsum(run, axis=1) - run
    before = jnp.cumsum(run, axis=0) - run
    group = -(-jnp.sum(run, axis=0) // EXPERT_ROWS) * EXPERT_ROWS
    ends = jnp.cumsum(group)
    sorted_off = (ends - group)[None, :] + before
    local_dest = jnp.sum(onehot * (local_off[:, None, :] + csum - 1), axis=2)
    runs = jnp.concatenate([local_off, sorted_off, run // RUN_ROWS], axis=1)[:, None, :]
    tile_start = jnp.arange(_sorted_rows(n, tile_rows) // EXPERT_ROWS, dtype=jnp.int32) * EXPERT_ROWS
    tile_expert = jnp.sum((tile_start[:, None] >= ends[None, :]).astype(jnp.int32), axis=1)
    return _RoutePlan(local_dest.astype(jnp.int32), runs.astype(jnp.int32),
                      jnp.minimum(tile_expert, N_EXPERTS - 1).astype(jnp.int32),
                      (ends[-1:] // EXPERT_ROWS).astype(jnp.int32))


def _start_runs(runs_ref, copy):
    for e in range(N_EXPERTS):
        local, dest, chunks = runs_ref[0, e], runs_ref[0, N_EXPERTS + e], runs_ref[0, 2 * N_EXPERTS + e]

        def start(j, carry, local=local, dest=dest):
            copy(pl.multiple_of(local + j * RUN_ROWS, RUN_ROWS),
                 pl.multiple_of(dest + j * RUN_ROWS, RUN_ROWS)).start()
            return carry

        lax.fori_loop(0, chunks, start, 0)


def _wait_runs(runs_ref, copy):
    total = 0
    for e in range(N_EXPERTS):
        total = total + runs_ref[0, 2 * N_EXPERTS + e]

    def wait(j, carry):
        copy(0, 0).wait()
        return carry

    lax.fori_loop(0, total, wait, 0)


def _dispatch_kernel(runs_ref, dest_ref, x_ref, xs_in_ref, xs_ref, buf_ref, sem):
    del xs_in_ref
    row = lax.broadcasted_iota(jnp.int32, (buf_ref.shape[0], x_ref.shape[0]), 0)
    pick = jnp.where(row == dest_ref[0:1, :], 1.0, jnp.where(row == dest_ref[1:2, :], 1.0, 0.0))
    buf_ref[...] = jnp.dot(pick.astype(BF16), x_ref[...].astype(BF16),
                           preferred_element_type=F32).astype(BF16)

    def copy(local, dest):
        return pltpu.make_async_copy(buf_ref.at[pl.ds(local, RUN_ROWS)], xs_ref.at[pl.ds(dest, RUN_ROWS)], sem)

    _start_runs(runs_ref, copy)
    _wait_runs(runs_ref, copy)


def _dispatch(x, plan, tile_rows):
    n, d = x.shape
    n_tiles = n // tile_rows
    sorted_rows = _sorted_rows(n, tile_rows)
    zeros = jnp.zeros((sorted_rows, d), BF16)
    return pl.pallas_call(
        _dispatch_kernel,
        grid=(n_tiles,),
        in_specs=[pl.BlockSpec((None, 1, 3 * N_EXPERTS), lambda i: (i, 0, 0), memory_space=pltpu.SMEM),
                  pl.BlockSpec((None, 2, tile_rows), lambda i: (i, 0, 0)),
                  pl.BlockSpec((tile_rows, d), lambda i: (i, 0)),
                  pl.BlockSpec(memory_space=pl.ANY)],
        out_specs=pl.BlockSpec(memory_space=pl.ANY),
        out_shape=jax.ShapeDtypeStruct((sorted_rows, d), BF16),
        scratch_shapes=[pltpu.VMEM((_compact_rows(tile_rows), d), BF16), pltpu.SemaphoreType.DMA(())],
        input_output_aliases={3: 0},
        name="moe_dispatch",
        compiler_params=_cparams(("arbitrary",)),
    )(plan.runs, plan.local_dest.reshape(n_tiles, 2, tile_rows), x, zeros)


def _expert_kernel(te_ref, used_ref, xs_ref, wg_ref, wu_ref, wd_ref, o_ref, acc_ref):
    del te_ref
    t = pl.program_id(0)
    j = pl.program_id(1)
    last = pl.num_programs(1) - 1

    @pl.when(t < used_ref[0])
    def _():
        piece = _swiglu_piece(xs_ref[...], wg_ref[...], wu_ref[...], wd_ref[...])

        @pl.when(j == 0)
        def _():
            acc_ref[...] = piece

        @pl.when(jnp.logical_and(j > 0, j < last))
        def _():
            acc_ref[...] += piece

        @pl.when(j == last)
        def _():
            o_ref[...] = (acc_ref[...] + piece).astype(o_ref.dtype)

    @pl.when(jnp.logical_and(t >= used_ref[0], j == last))
    def _():
        o_ref[...] = jnp.zeros(o_ref.shape, o_ref.dtype)


def _experts(xs, plan, w_gu, w_down):
    r, d = xs.shape
    d_ff = w_down.shape[1]
    piece = d_ff // FFN_SPLIT
    layer = w_gu.layer
    assert FFN_SPLIT >= 2
    grid_spec = pltpu.PrefetchScalarGridSpec(
        num_scalar_prefetch=2,
        grid=(r // EXPERT_ROWS, FFN_SPLIT),
        in_specs=[pl.BlockSpec((EXPERT_ROWS, d), lambda t, j, te, used: (t, 0)),
                  pl.BlockSpec((None, None, d, piece), lambda t, j, te, used: (layer, te[t], 0, j)),
                  pl.BlockSpec((None, None, d, piece), lambda t, j, te, used: (layer, te[t], 0, FFN_SPLIT + j)),
                  pl.BlockSpec((None, None, piece, d), lambda t, j, te, used: (layer, te[t], j, 0))],
        out_specs=pl.BlockSpec((EXPERT_ROWS, d), lambda t, j, te, used: (t, 0)),
        scratch_shapes=[pltpu.VMEM((EXPERT_ROWS, d), F32)],
    )
    return pl.pallas_call(
        _expert_kernel,
        grid_spec=grid_spec,
        out_shape=jax.ShapeDtypeStruct((r, d), BF16),
        name="moe_experts",
        compiler_params=_cparams(("parallel", "arbitrary")),
    )(plan.tile_expert, plan.n_used, xs, w_gu.array, w_gu.array, w_down.array)


def _combine_kernel(runs_ref, next_runs_ref, x_ref, dest_ref, gate_ref, ys_ref, g_ref, b_ref, o_ref,
                    buf_ref, sem, *, alpha):
    i = pl.program_id(0)
    half = lax.rem(i, 2)

    def copy_into(which):
        def copy(local, src):
            return pltpu.make_async_copy(ys_ref.at[pl.ds(src, RUN_ROWS)],
                                         buf_ref.at[which, pl.ds(local, RUN_ROWS)], sem.at[which])
        return copy

    @pl.when(i == 0)
    def _():
        buf_ref[...] = jnp.zeros(buf_ref.shape, buf_ref.dtype)
        _start_runs(runs_ref, copy_into(0))

    @pl.when(i + 1 < pl.num_programs(0))
    def _():
        _start_runs(next_runs_ref, copy_into(1 - half))

    _wait_runs(runs_ref, copy_into(half))
    y = buf_ref[half]
    col = lax.broadcasted_iota(jnp.int32, (x_ref.shape[0], y.shape[0]), 1)
    gates = gate_ref[...]
    mixed = None
    for slot in range(2):
        pick = jnp.where(col == dest_ref[:, slot:slot + 1], 1.0, 0.0).astype(BF16)
        term = gates[:, slot:slot + 1] * jnp.dot(pick, y, preferred_element_type=F32)
        mixed = term if mixed is None else mixed + term
    o_ref[...] = _layer_norm(alpha * x_ref[...] + mixed, g_ref[...], b_ref[...])


def _combine(x, plan, gates_t, ys, ln_g, ln_b, *, tile_rows, alpha):
    n, d = x.shape
    n_tiles = n // tile_rows
    row_spec = pl.BlockSpec((tile_rows, d), lambda i: (i, 0))
    pair_spec = pl.BlockSpec((tile_rows, 2), lambda i: (i, 0))
    dest_cols = plan.local_dest.reshape(n_tiles, 2, tile_rows).transpose(0, 2, 1).reshape(n, 2)
    runs_shape = (None, 1, 3 * N_EXPERTS)
    return pl.pallas_call(
        functools.partial(_combine_kernel, alpha=alpha),
        grid=(n_tiles,),
        in_specs=[pl.BlockSpec(runs_shape, lambda i: (i, 0, 0), memory_space=pltpu.SMEM),
                  pl.BlockSpec(runs_shape, lambda i: (jnp.minimum(i + 1, n_tiles - 1), 0, 0),
                               memory_space=pltpu.SMEM),
                  row_spec, pair_spec, pair_spec,
                  pl.BlockSpec(memory_space=pl.ANY),
                  _resident(ln_g), _resident(ln_b)],
        out_specs=row_spec,
        out_shape=jax.ShapeDtypeStruct((n, d), F32),
        scratch_shapes=[pltpu.VMEM((2, _compact_rows(tile_rows), d), ys.dtype),
                        pltpu.SemaphoreType.DMA((2,))],
        name="moe_combine",
        compiler_params=_cparams(("arbitrary",)),
    )(plan.runs, plan.runs, x, dest_cols, gates_t, ys, ln_g, ln_b)


def _moe_layer(x, w_router_t, w_gu, w_down, ln_g, ln_b, *, alpha):
    n, _ = x.shape
    tile_rows = min(MOVE_ROWS, n)
    assert n % tile_rows == 0
    idx, gates = _router(x, w_router_t)
    plan = _route_plan(idx, tile_rows)
    xs = _dispatch(x, plan, tile_rows)
    ys = _experts(xs, plan, w_gu, w_down)
    return _combine(x, plan, gates.T, ys, ln_g, ln_b, tile_rows=tile_rows, alpha=alpha)


def _dot_nt(a, b):
    return lax.dot_general(a, b, (((1,), (1,)), ((), ())), preferred_element_type=F32)


def _kv_kernel(x_ref, w_ref, wvt_ref, k_ref, v_ref, kb_ref, vt_ref):
    qk_w = k_ref.shape[-1]
    xb = x_ref[...].astype(BF16)
    kv = jnp.dot(xb, w_ref[...], preferred_element_type=F32)
    k = kv[:, :qk_w]
    k_ref[...] = k
    v_ref[...] = kv[:, qk_w:]
    kb_ref[...] = k.astype(BF16)
    vt_ref[...] = _dot_nt(wvt_ref[...], xb).astype(BF16)


def _kv_proj(x, w_kv, w_vt):
    bsz, t, d = x.shape
    qk_w = w_kv.shape[1] // 2
    rows = min(ATT_K, t)
    assert t % rows == 0
    row_spec = pl.BlockSpec((None, rows, qk_w), lambda b, i: (b, i, 0))
    return pl.pallas_call(
        _kv_kernel,
        grid=(bsz, t // rows),
        in_specs=[pl.BlockSpec((None, rows, d), lambda b, i: (b, i, 0)),
                  _resident(w_kv), _resident(w_vt)],
        out_specs=[row_spec, row_spec, row_spec,
                   pl.BlockSpec((None, None, qk_w, rows), lambda b, i: (b, i, 0, 0))],
        out_shape=[jax.ShapeDtypeStruct((bsz, t, qk_w), F32)] * 2
        + [jax.ShapeDtypeStruct((bsz, t, qk_w), BF16),
           jax.ShapeDtypeStruct((bsz, t // rows, qk_w, rows), BF16)],
        name="kv_proj",
        compiler_params=_cparams(("parallel", "parallel")),
    )(x, w_kv, w_vt)


def _t5_bucket(rel):
    nb = NUM_BUCKETS // 2
    max_exact = nb // 2
    ret = (rel > 0).astype(jnp.int32) * nb
    n = jnp.abs(rel)
    nf = jnp.maximum(n, 1).astype(jnp.float32)
    large = max_exact + (jnp.log(nf / max_exact) / math.log(MAX_DISTANCE / max_exact)
                         * (nb - max_exact)).astype(jnp.int32)
    large = jnp.minimum(large, nb - 1)
    return ret + jnp.where(n < max_exact, n, large)


def _bucket_table(q_pos, k_pos, visible=None):
    bucket = _t5_bucket(k_pos[:, None] - q_pos[None, :])
    return bucket if visible is None else jnp.where(visible, bucket, -1)


def _bias_kernel(far_ref, rb_ref, bucket_ref, o_ref):
    h = pl.program_id(0)
    bucket = bucket_ref[...]
    far = rb_ref[far_ref[0], h]
    out = jnp.zeros(bucket.shape, F32)
    for b in range(NUM_BUCKETS):
        out = jnp.where(bucket == b, rb_ref[b, h] - far, out)
    o_ref[...] = jnp.where(bucket < 0, NEG_INF, out * LOG2_E)


def _bias_tables(rel_bias, buckets):
    n_heads2 = rel_bias.shape[1]
    far_bucket = _t5_bucket(jnp.full((1,), -MAX_DISTANCE, jnp.int32))
    return pl.pallas_call(
        _bias_kernel,
        grid=(n_heads2,),
        in_specs=[pl.BlockSpec(memory_space=pltpu.SMEM), pl.BlockSpec(memory_space=pltpu.SMEM),
                  _resident(buckets)],
        out_specs=pl.BlockSpec((None,) + buckets.shape, lambda h: (h, 0, 0)),
        out_shape=jax.ShapeDtypeStruct((n_heads2,) + buckets.shape, F32),
        name="bias_tables",
        compiler_params=_cparams(("parallel",)),
    )(far_bucket, rel_bias.astype(F32), buckets)


def _att_init(x_ref, wqt_ref, qa_ref, qb_ref, m_ref, l_ref, acc_ref):
    qt = _dot_nt(wqt_ref[...], x_ref[...].astype(BF16)) * (HEAD_DIM ** -0.5 * LOG2_E)
    row = lax.broadcasted_iota(jnp.int32, (LANES, qt.shape[1]), 0)
    for h in range(qa_ref.shape[0]):
        slab = qt[h * LANES:(h + 1) * LANES, :]
        qa_ref[h] = jnp.where(row < HEAD_DIM, slab, 0.0).astype(BF16)
        qb_ref[h] = jnp.where(row >= HEAD_DIM, slab, 0.0).astype(BF16)
    m_ref[...] = jnp.full(m_ref.shape, NEG_INF, F32)
    l_ref[...] = jnp.zeros(l_ref.shape, F32)
    acc_ref[...] = jnp.zeros(acc_ref.shape, F32)


def _att_block(qa_ref, qb_ref, k, vt_slab, bias, m_ref, l_ref, acc_ref):
    n_heads = 2 * qa_ref.shape[0]

    def scores(hh):
        h = hh // 2
        q_ref = qa_ref if hh % 2 == 0 else qb_ref
        s = jnp.dot(k[:, h * LANES:(h + 1) * LANES], q_ref[h], preferred_element_type=F32)
        return s if bias is None else s + bias(hh)

    s_of, p_of, scale_of = {}, {}, {}

    def softmax(hh):
        s = s_of.pop(hh)
        m_old = m_ref[hh]
        m_new = jnp.maximum(m_old, jnp.max(s, axis=0, keepdims=True))
        scale = jnp.exp2(m_old - m_new)
        p = jnp.exp2(s - m_new)
        l_ref[hh] = scale * l_ref[hh] + jnp.sum(p, axis=0, keepdims=True)
        m_ref[hh] = m_new
        p_of[hh] = p.astype(BF16)
        scale_of[hh] = scale

    def accumulate(hh):
        acc_ref[hh] = scale_of.pop(hh) * acc_ref[hh] + jnp.dot(vt_slab(hh // 2), p_of.pop(hh),
                                                               preferred_element_type=F32)

    n_groups = n_heads // HEAD_GROUP
    group = lambda t: range(t * HEAD_GROUP, (t + 1) * HEAD_GROUP)
    for t in range(-2, n_groups):
        if t + 2 < n_groups:
            for hh in group(t + 2):
                s_of[hh] = scores(hh)
        if 0 <= t + 1 < n_groups:
            for hh in group(t + 1):
                softmax(hh)
        if t >= 0:
            for hh in group(t):
                accumulate(hh)


def _att_finish(x_ref, lam_ref, sg_ref, wout_ref, g_ref, b_ref, o_ref, l_ref, acc_ref, heads_ref,
                *, lam_init, alpha):
    lp = lam_ref[...]
    lam = (jnp.exp(jnp.sum(lp[0:1] * lp[1:2], axis=-1, keepdims=True))
           - jnp.exp(jnp.sum(lp[2:3] * lp[3:4], axis=-1, keepdims=True)) + lam_init)
    for h in range(acc_ref.shape[0] // 2):
        a = acc_ref[2 * h] * (1.0 / l_ref[2 * h]) - (lam / l_ref[2 * h + 1]) * acc_ref[2 * h + 1]
        r = a * lax.rsqrt(jnp.mean(a * a, axis=0, keepdims=True) + LN_EPS) * sg_ref[...]
        heads_ref[h * LANES:(h + 1) * LANES, :] = r * (1.0 - lam_init)
    heads = jnp.transpose(heads_ref[...]).astype(BF16)
    y = jnp.dot(heads, wout_ref[...], preferred_element_type=F32)
    o_ref[...] = _layer_norm(alpha * x_ref[...] + y, g_ref[...], b_ref[...])


def _att_scratch(q_rows, width):
    n_slabs = width // LANES
    return [pltpu.VMEM((n_slabs, LANES, q_rows), BF16), pltpu.VMEM((n_slabs, LANES, q_rows), BF16),
            pltpu.VMEM((2 * n_slabs, 1, q_rows), F32), pltpu.VMEM((2 * n_slabs, 1, q_rows), F32),
            pltpu.VMEM((2 * n_slabs, LANES, q_rows), F32), pltpu.VMEM((width, q_rows), F32)]


STEP_FAR_WIDE = 0
STEP_FAR_ONE = 1
STEP_NEAR = 2
FAR_TILES = 4


def _prompt_att_kernel(qi_ref, kind_ref, first_ref, wide_ref, ka_ref_idx, kb_ref_idx,
                       x_ref, kw_ref, vtw_ref, ka_ref, vta_ref, kb_ref, vtb_ref, wqt_ref, bias_ref, lam_ref,
                       sg_ref, wout_ref, g_ref, b_ref, o_ref, qa_ref, qb_ref, m_ref, l_ref, acc_ref,
                       heads_ref, *, lam_init, alpha):
    del wide_ref, ka_ref_idx, kb_ref_idx
    p = pl.program_id(1)
    kind = kind_ref[p]
    tk = ka_ref.shape[0]
    state = (m_ref, l_ref, acc_ref)

    def slab(ref, h):
        return ref[h * LANES:(h + 1) * LANES, :]

    @pl.when(first_ref[p] == 1)
    def _():
        _att_init(x_ref, wqt_ref, qa_ref, qb_ref, *state)

    @pl.when(kind == STEP_FAR_WIDE)
    def _():
        def vt_slab(h):
            return jnp.concatenate([vtw_ref[c, h * LANES:(h + 1) * LANES, :]
                                    for c in range(vtw_ref.shape[0])], axis=1)
        _att_block(qa_ref, qb_ref, kw_ref[...], vt_slab, None, *state)

    @pl.when(kind == STEP_FAR_ONE)
    def _():
        _att_block(qa_ref, qb_ref, ka_ref[...], lambda h: slab(vta_ref, h), None, *state)

    @pl.when(kind == STEP_NEAR)
    def _():
        hide = jnp.where(qi_ref[p] == 0, NEG_INF, 0.0)

        def bias(hh):
            return jnp.concatenate([bias_ref[hh, 0:tk, :] + hide, bias_ref[hh, tk:2 * tk, :]], axis=0)

        def vt_slab(h):
            return jnp.concatenate([slab(vta_ref, h), slab(vtb_ref, h)], axis=1)

        k = jnp.concatenate([ka_ref[...], kb_ref[...]], axis=0)
        _att_block(qa_ref, qb_ref, k, vt_slab, bias, *state)
        _att_finish(x_ref, lam_ref, sg_ref, wout_ref, g_ref, b_ref, o_ref, l_ref, acc_ref, heads_ref,
                    lam_init=lam_init, alpha=alpha)


def _prompt_bias(rel_bias, tq):
    pos = jnp.arange(tq, dtype=jnp.int32)
    visible = pos[:, None] < (pos[None, :] // CHUNK + 1) * CHUNK
    buckets = jnp.concatenate([_bucket_table(pos + tq, pos), _bucket_table(pos, pos, visible)], axis=0)
    return _bias_tables(rel_bias, buckets)


def _prompt_steps(n_q):
    steps = []
    wide = ka = kb = 0
    for i in range(n_q):
        n_far = max(i - 1, 0)
        first = 1
        for c in range(n_far // FAR_TILES):
            wide = c
            steps.append((i, STEP_FAR_WIDE, first, wide, ka, kb))
            first = 0
        for j in range(n_far - n_far % FAR_TILES, n_far):
            ka = j
            steps.append((i, STEP_FAR_ONE, first, wide, ka, kb))
            first = 0
        ka, kb = max(i - 1, 0), i
        steps.append((i, STEP_NEAR, first, wide, ka, kb))
    return [jnp.array(col, jnp.int32) for col in zip(*steps)]


def _prompt_attention(x, kb, vt, bias, w_qt, lam_p, subln_g, w_out, ln_g, ln_b, *, lam_init, alpha):
    bsz, s_len, d = x.shape
    width = kb.shape[-1]
    tq = vt.shape[-1]
    n_q = s_len // tq
    assert s_len % tq == 0 and tq % CHUNK == 0 and tq >= MAX_DISTANCE
    assert n_q % FAR_TILES == 0 or n_q <= FAR_TILES
    steps = _prompt_steps(n_q)
    wide_tiles = min(FAR_TILES, n_q)
    x_spec = pl.BlockSpec((None, tq, d), lambda b, p, qi, kind, first, wide, ka, kb: (b, qi[p], 0))
    grid_spec = pltpu.PrefetchScalarGridSpec(
        num_scalar_prefetch=len(steps),
        grid=(bsz, steps[0].shape[0]),
        in_specs=[x_spec,
                  pl.BlockSpec((None, wide_tiles * tq, width),
                               lambda b, p, qi, kind, first, wide, ka, kb: (b, wide[p], 0)),
                  pl.BlockSpec((None, wide_tiles, width, tq),
                               lambda b, p, qi, kind, first, wide, ka, kb: (b, wide[p], 0, 0)),
                  pl.BlockSpec((None, tq, width), lambda b, p, qi, kind, first, wide, ka, kb: (b, ka[p], 0)),
                  pl.BlockSpec((None, None, width, tq),
                               lambda b, p, qi, kind, first, wide, ka, kb: (b, ka[p], 0, 0)),
                  pl.BlockSpec((None, tq, width), lambda b, p, qi, kind, first, wide, ka, kb: (b, kb[p], 0)),
                  pl.BlockSpec((None, None, width, tq),
                               lambda b, p, qi, kind, first, wide, ka, kb: (b, kb[p], 0, 0)),
                  _resident(w_qt), _resident(bias), _resident(lam_p), _resident(subln_g), _resident(w_out),
                  _resident(ln_g), _resident(ln_b)],
        out_specs=x_spec,
        scratch_shapes=_att_scratch(tq, width),
    )
    return pl.pallas_call(
        functools.partial(_prompt_att_kernel, lam_init=lam_init, alpha=alpha),
        grid_spec=grid_spec,
        out_shape=jax.ShapeDtypeStruct((bsz, s_len, d), F32),
        name="prompt_attention",
        compiler_params=_cparams(("parallel", "arbitrary")),
    )(*steps, x, kb, vt, kb, vt, kb, vt, _operand(w_qt), bias, lam_p, subln_g, _operand(w_out), ln_g, ln_b)


def _sample_att_kernel(x_ref, ck_ref, cv_ref, kn_ref, vtn_ref, wqt_ref, bias_ref, lam_ref, sg_ref,
                       wout_ref, g_ref, b_ref, o_ref, qa_ref, qb_ref, m_ref, l_ref, acc_ref, heads_ref,
                       *, lam_init, alpha):
    j = pl.program_id(1)
    n_cache = pl.num_programs(1) - 1
    tk = ck_ref.shape[0]
    t_new = kn_ref.shape[0]
    state = (m_ref, l_ref, acc_ref)

    def near_bias(hh):
        return bias_ref[hh, 0:tk, :]

    def new_bias(hh):
        return bias_ref[hh, tk:tk + t_new, :]

    def cached_vt_slab(h):
        return jnp.transpose(cv_ref[:, h * LANES:(h + 1) * LANES]).astype(BF16)

    def new_vt_slab(h):
        return vtn_ref[h * LANES:(h + 1) * LANES, :]

    @pl.when(j == 0)
    def _():
        _att_init(x_ref, wqt_ref, qa_ref, qb_ref, *state)

    @pl.when(j < n_cache - 1)
    def _():
        _att_block(qa_ref, qb_ref, ck_ref[...].astype(BF16), cached_vt_slab, None, *state)

    @pl.when(j == n_cache - 1)
    def _():
        _att_block(qa_ref, qb_ref, ck_ref[...].astype(BF16), cached_vt_slab, near_bias, *state)

    @pl.when(j == n_cache)
    def _():
        _att_block(qa_ref, qb_ref, kn_ref[...], new_vt_slab, new_bias, *state)
        _att_finish(x_ref, lam_ref, sg_ref, wout_ref, g_ref, b_ref, o_ref, l_ref, acc_ref, heads_ref,
                    lam_init=lam_init, alpha=alpha)


def _sample_bias(rel_bias, past, t):
    tk = min(CACHE_K, past)
    q_pos = past + jnp.arange(t, dtype=jnp.int32)
    buckets = jnp.concatenate([_bucket_table(q_pos, past - tk + jnp.arange(tk, dtype=jnp.int32)),
                               _bucket_table(q_pos, q_pos)], axis=0)
    return _bias_tables(rel_bias, buckets)


def _sample_attention(x, cache_k, cache_v, kb, vt, bias, w_qt, lam_p, subln_g, w_out, ln_g, ln_b,
                      *, lam_init, alpha):
    bsz, t, d = x.shape
    past = cache_k.shape[1]
    width = kb.shape[-1]
    tk = min(CACHE_K, past)
    assert past % tk == 0 and tk >= MAX_DISTANCE and vt.shape[1] == 1
    n_cache = past // tk
    cache_spec = pl.BlockSpec((None, tk, width), lambda b, j: (b, jnp.minimum(j, n_cache - 1), 0))
    x_spec = pl.BlockSpec((None, t, d), lambda b, j: (b, 0, 0))
    return pl.pallas_call(
        functools.partial(_sample_att_kernel, lam_init=lam_init, alpha=alpha),
        grid=(bsz, n_cache + 1),
        in_specs=[x_spec, cache_spec, cache_spec,
                  pl.BlockSpec((None, t, width), lambda b, j: (b, 0, 0)),
                  pl.BlockSpec((None, None, width, t), lambda b, j: (b, 0, 0, 0)),
                  _resident(w_qt), _resident(bias), _resident(lam_p), _resident(subln_g), _resident(w_out),
                  _resident(ln_g), _resident(ln_b)],
        out_specs=x_spec,
        out_shape=jax.ShapeDtypeStruct((bsz, t, d), F32),
        scratch_shapes=_att_scratch(t, width),
        name="sample_attention",
        compiler_params=_cparams(("parallel", "arbitrary")),
    )(x, cache_k, cache_v, kb, vt, _operand(w_qt), bias, lam_p, subln_g, _operand(w_out), ln_g, ln_b)


def _trunk(x, cache_k, cache_v, w, *, depth, emit_v):
    bsz, t, d = x.shape
    n = bsz * t
    n_a = depth // 2
    alpha = (2 * depth) ** 0.25
    chunk = min(t, SGU_CHUNK)
    h = x.reshape(n, d)
    sgu_rows = []
    k_new = v_new = kb = vt = None
    for i in range(depth):
        if i < n_a:
            h, v_rows = _sgu_layer(
                h, _Stacked(w["a_w_in"], i), w["a_ln_g"][i], w["a_ln_b"][i], w["a_w_s"][i][:, :chunk, :chunk],
                w["a_b_s"][i][:, :chunk].T, _Stacked(w["a_w_out"], i), w["ln_mix_g"][i], w["ln_mix_b"][i],
                chunk=chunk, alpha=alpha, emit_v=emit_v)
            sgu_rows.append(v_rows)
        else:
            if k_new is None:
                k_new, v_new, kb, vt = _kv_proj(h.reshape(bsz, t, d), w["w_kv"], w["w_vt"])
                if cache_k is None:
                    bias = _prompt_bias(w["rel_bias"], vt.shape[-1])
                else:
                    bias = _sample_bias(w["rel_bias"], cache_k.shape[1], t)
            j = i - n_a
            lam_init = 0.8 - 0.6 * math.exp(-0.3 * i)
            args = (kb, vt, bias, _Stacked(w["b_w_qt"], j), w["b_lam"][j], w["b_subln_g"][j],
                    _Stacked(w["b_w_out"], j), w["ln_mix_g"][i], w["ln_mix_b"][i])
            if cache_k is None:
                h3 = _prompt_attention(h.reshape(bsz, t, d), *args, lam_init=lam_init, alpha=alpha)
            else:
                past = cache_k.shape[1]
                h3 = _sample_attention(h.reshape(bsz, t, d), cache_k.reshape(bsz, past, -1),
                                       cache_v.reshape(bsz, past, -1), *args, lam_init=lam_init, alpha=alpha)
            h = h3.reshape(n, d)
        if i % 2 == 0:
            h = _ffn_layer(h, _Stacked(w["ffn_w_gu"], i // 2), _Stacked(w["ffn_w_down"], i // 2),
                           w["ln_ffn_g"][i], w["ln_ffn_b"][i], alpha=alpha)
        else:
            h = _moe_layer(h, w["moe_w_router_t"][i // 2], _Stacked(w["moe_w_gu"], i // 2),
                           _Stacked(w["moe_w_down"], i // 2), w["ln_ffn_g"][i], w["ln_ffn_b"][i], alpha=alpha)
    return h.reshape(bsz, t, d), k_new, v_new, sgu_rows


def kernel(x_prompt, x_sample, cache_k, cache_v, a_w_in, a_ln_g, a_ln_b, a_w_s, a_b_s, a_w_out, w_kv, b_w_q,
           b_lam, b_subln_g, b_w_out, rel_bias, ln_mix_g, ln_mix_b, ln_ffn_g, ln_ffn_b, ffn_w_gu, ffn_w_down,
           moe_w_router, moe_w_gu, moe_w_down):
    depth = ln_mix_g.shape[0]
    n_heads2 = rel_bias.shape[1]
    row = lambda a: a[:, None, :]
    w = dict(
        a_w_in=a_w_in.astype(BF16), a_ln_g=row(a_ln_g), a_ln_b=row(a_ln_b), a_w_s=a_w_s, a_b_s=a_b_s,
        a_w_out=a_w_out.astype(BF16), w_kv=w_kv.astype(BF16),
        w_vt=jnp.transpose(w_kv[:, w_kv.shape[1] // 2:]).astype(BF16),
        b_w_qt=jnp.swapaxes(b_w_q, 1, 2).astype(BF16), b_lam=b_lam,
        b_subln_g=b_subln_g[:, :, None], b_w_out=b_w_out.astype(BF16), rel_bias=rel_bias,
        ln_mix_g=row(ln_mix_g), ln_mix_b=row(ln_mix_b), ln_ffn_g=row(ln_ffn_g), ln_ffn_b=row(ln_ffn_b),
        ffn_w_gu=ffn_w_gu.astype(BF16), ffn_w_down=ffn_w_down.astype(BF16),
        moe_w_router_t=jnp.swapaxes(moe_w_router, 1, 2), moe_w_gu=moe_w_gu.astype(BF16),
        moe_w_down=moe_w_down.astype(BF16))
    y_p, k_p, v_p, _ = _trunk(x_prompt, None, None, w, depth=depth, emit_v=False)
    y_s, k_s, v_s, sgu = _trunk(x_sample, cache_k, cache_v, w, depth=depth, emit_v=True)
    bp, sp, _ = x_prompt.shape
    bs, ss, _ = x_sample.shape
    n_heads = n_heads2 // 2
    d_sgu = a_w_out.shape[1]
    return (y_p, y_s,
            k_p.reshape(bp, sp, n_heads2, HEAD_DIM), v_p.reshape(bp, sp, n_heads, 2 * HEAD_DIM),
            k_s.reshape(bs, ss, n_heads2, HEAD_DIM), v_s.reshape(bs, ss, n_heads, 2 * HEAD_DIM),
            jnp.stack([r.reshape(bs, ss, d_sgu) for r in sgu], axis=0))
```

```python
import functools
import math
from typing import NamedTuple

import jax
import jax.numpy as jnp
from jax import lax
from jax.experimental import pallas as pl
from jax.experimental.pallas import tpu as pltpu

F32 = jnp.float32
BF16 = jnp.bfloat16

CHUNK = 64
SGU_CHUNK = 128
SGU_GROUPS = 4
HEAD_DIM = 64
NUM_BUCKETS = 32
MAX_DISTANCE = 128
N_EXPERTS = 8
LN_EPS = 1e-5
NEG_INF = -1e30
SQRT_HALF = math.sqrt(0.5)
LOG2_E = math.log2(math.e)

LANES = 128
VMEM_LIMIT_BYTES = 56 * 1024 * 1024

SGU_ROWS = 256
FFN_ROWS = 512
FFN_SPLIT = 2
ROUTE_ROWS = 1024
MOVE_ROWS = 512
RUN_ROWS = 16
EXPERT_ROWS = 512
ATT_K = 256
CACHE_K = 512
HEAD_GROUP = 2


def _cparams(semantics):
    return pltpu.CompilerParams(dimension_semantics=semantics, vmem_limit_bytes=VMEM_LIMIT_BYTES)


class _Stacked(NamedTuple):
    array: jax.Array
    layer: int

    @property
    def shape(self):
        return self.array.shape[1:]


def _operand(a):
    return a.array if isinstance(a, _Stacked) else a


def _resident(a):
    nd = len(a.shape)
    if isinstance(a, _Stacked):
        return pl.BlockSpec((None,) + a.shape, lambda *_: (a.layer,) + (0,) * nd,
                            pipeline_mode=pl.Buffered(1))
    return pl.BlockSpec(a.shape, lambda *_: (0,) * nd, pipeline_mode=pl.Buffered(1))


def _layer_norm(z, g, b):
    mu = jnp.mean(z, axis=-1, keepdims=True)
    zc = z - mu
    var = jnp.mean(zc * zc, axis=-1, keepdims=True)
    return zc * lax.rsqrt(var + LN_EPS) * g + b


def _gelu_exact(h):
    return 0.5 * h * (1.0 + lax.erf(h * SQRT_HALF))


def _sgu_kernel(x_ref, win_ref, lg_ref, lb_ref, ws_ref, bs_ref, wout_ref, mg_ref, mb_ref,
                o_ref, *rest, chunk, alpha, emit_v):
    if emit_v:
        v_ref, gate_ref = rest
    else:
        (gate_ref,) = rest
    rows = x_ref.shape[0]
    d_sgu = wout_ref.shape[0]
    gw = d_sgu // SGU_GROUPS
    x = x_ref[...]
    xb = x.astype(BF16)
    v = _gelu_exact(jnp.dot(xb, win_ref[:, d_sgu:], preferred_element_type=F32))
    vn = _layer_norm(v, lg_ref[...], lb_ref[...])
    if emit_v:
        v_ref[...] = vn
    vb = vn.astype(BF16)
    u = _gelu_exact(jnp.dot(xb, win_ref[:, :d_sgu], preferred_element_type=F32))
    tril = (lax.broadcasted_iota(jnp.int32, (chunk, chunk), 0)
            >= lax.broadcasted_iota(jnp.int32, (chunk, chunk), 1))
    for g in range(SGU_GROUPS):
        wg = jnp.where(tril, ws_ref[g], 0.0).astype(BF16)
        bcol = bs_ref[:, g:g + 1]
        for c in range(rows // chunk):
            r0, r1, c0, c1 = c * chunk, (c + 1) * chunk, g * gw, (g + 1) * gw
            mixed = jnp.dot(wg, vb[r0:r1, c0:c1], preferred_element_type=F32) + bcol
            gate_ref[r0:r1, c0:c1] = (u[r0:r1, c0:c1] * mixed).astype(BF16)
    y = jnp.dot(gate_ref[...], wout_ref[...], preferred_element_type=F32)
    o_ref[...] = _layer_norm(alpha * x + y, mg_ref[...], mb_ref[...])


def _sgu_layer(x, w_in, ln_g, ln_b, w_s, b_s, w_out, mix_g, mix_b, *, chunk, alpha, emit_v):
    n, d = x.shape
    d_sgu = w_out.shape[0]
    rows = min(SGU_ROWS, n)
    assert n % rows == 0 and rows % chunk == 0
    row_spec = pl.BlockSpec((rows, d), lambda i: (i, 0))
    out_shape = [jax.ShapeDtypeStruct((n, d), F32)]
    out_specs = [row_spec]
    if emit_v:
        out_shape.append(jax.ShapeDtypeStruct((n, d_sgu), F32))
        out_specs.append(pl.BlockSpec((rows, d_sgu), lambda i: (i, 0)))
    res = pl.pallas_call(
        functools.partial(_sgu_kernel, chunk=chunk, alpha=alpha, emit_v=emit_v),
        grid=(n // rows,),
        in_specs=[row_spec] + [_resident(a) for a in (w_in, ln_g, ln_b, w_s, b_s, w_out, mix_g, mix_b)],
        out_specs=out_specs,
        out_shape=out_shape,
        scratch_shapes=[pltpu.VMEM((rows, d_sgu), BF16)],
        name="sgu_layer",
        compiler_params=_cparams(("parallel",)),
    )(x, _operand(w_in), ln_g, ln_b, w_s, b_s, _operand(w_out), mix_g, mix_b)
    return res if emit_v else (res[0], None)


def _swiglu_piece(xb, wg, wu, wd):
    g = jnp.dot(xb, wg, preferred_element_type=F32)
    u = jnp.dot(xb, wu, preferred_element_type=F32)
    a = (g * jax.nn.sigmoid(g) * u).astype(BF16)
    return jnp.dot(a, wd, preferred_element_type=F32)


def _ffn_kernel(x_ref, wgu_ref, wd_ref, g_ref, b_ref, o_ref, *, alpha):
    d_ff = wd_ref.shape[0]
    piece = d_ff // FFN_SPLIT
    x = x_ref[...]
    xb = x.astype(BF16)
    acc = None
    for j in range(FFN_SPLIT):
        t = _swiglu_piece(xb, wgu_ref[:, j * piece:(j + 1) * piece],
                          wgu_ref[:, d_ff + j * piece:d_ff + (j + 1) * piece],
                          wd_ref[j * piece:(j + 1) * piece, :])
        acc = t if acc is None else acc + t
    o_ref[...] = _layer_norm(alpha * x + acc, g_ref[...], b_ref[...])


def _ffn_layer(x, w_gu, w_down, ln_g, ln_b, *, alpha):
    n, d = x.shape
    rows = min(FFN_ROWS, n)
    assert n % rows == 0 and w_down.shape[0] % (FFN_SPLIT * LANES) == 0
    row_spec = pl.BlockSpec((rows, d), lambda i: (i, 0))
    return pl.pallas_call(
        functools.partial(_ffn_kernel, alpha=alpha),
        grid=(n // rows,),
        in_specs=[row_spec] + [_resident(a) for a in (w_gu, w_down, ln_g, ln_b)],
        out_specs=row_spec,
        out_shape=jax.ShapeDtypeStruct((n, d), F32),
        name="dense_ffn",
        compiler_params=_cparams(("parallel",)),
    )(x, _operand(w_gu), _operand(w_down), ln_g, ln_b)


def _router_kernel(x_ref, wr_ref, idx_ref, gate_ref):
    logits = lax.dot_general(wr_ref[...], x_ref[...], (((1,), (1,)), ((), ())),
                             precision=lax.Precision.HIGHEST, preferred_element_type=F32)
    ids = lax.broadcasted_iota(jnp.int32, logits.shape, 0)
    m1 = jnp.max(logits, axis=0, keepdims=True)
    i1 = jnp.min(jnp.where(logits == m1, ids, N_EXPERTS), axis=0, keepdims=True)
    rest = jnp.where(ids == i1, -jnp.inf, logits)
    m2 = jnp.max(rest, axis=0, keepdims=True)
    i2 = jnp.min(jnp.where(rest == m2, ids, N_EXPERTS), axis=0, keepdims=True)
    e2 = jnp.exp(m2 - m1)
    den = 1.0 + e2
    idx_ref[...] = jnp.concatenate([i1, i2], axis=0)
    gate_ref[...] = jnp.concatenate([1.0 / den, e2 / den], axis=0)


def _router(x, w_router_t):
    n, d = x.shape
    rows = min(ROUTE_ROWS, n)
    assert n % rows == 0
    return pl.pallas_call(
        _router_kernel,
        grid=(n // rows,),
        in_specs=[pl.BlockSpec((rows, d), lambda i: (i, 0)), _resident(w_router_t)],
        out_specs=[pl.BlockSpec((2, rows), lambda i: (0, i)), pl.BlockSpec((2, rows), lambda i: (0, i))],
        out_shape=[jax.ShapeDtypeStruct((2, n), jnp.int32), jax.ShapeDtypeStruct((2, n), F32)],
        name="moe_router",
        compiler_params=_cparams(("parallel",)),
    )(x, w_router_t)


class _RoutePlan(NamedTuple):
    local_dest: jax.Array
    runs: jax.Array
    tile_expert: jax.Array
    n_used: jax.Array


def _compact_rows(tile_rows):
    bound = 2 * tile_rows + N_EXPERTS * (RUN_ROWS - 1)
    return -(-bound // LANES) * LANES


def _sorted_rows(n_tokens, tile_rows):
    bound = 2 * n_tokens + (n_tokens // tile_rows) * N_EXPERTS * (RUN_ROWS - 1)
    return -(-bound // EXPERT_ROWS) * EXPERT_ROWS + N_EXPERTS * EXPERT_ROWS


def _route_plan(idx, tile_rows):
    n = idx.shape[1]
    n_tiles = n // tile_rows
    experts = idx.reshape(2, n_tiles, tile_rows).transpose(1, 0, 2).reshape(n_tiles, 2 * tile_rows)
    onehot = (experts[:, :, None] == jnp.arange(N_EXPERTS, dtype=jnp.int32)).astype(jnp.int32)
    order = jnp.arange(2 * tile_rows, dtype=jnp.int32)
    tri = (order[:, None] >= order[None, :]).astype(BF16)
    csum = jnp.einsum("uv,ive->iue", tri, onehot.astype(BF16),
                      preferred_element_type=F32).astype(jnp.int32)
    count = csum[:, -1, :]
    run = -(-count // RUN_ROWS) * RUN_ROWS
    local_off = jnp.cumsum(run, axis=1) - run
    before = jnp.cumsum(run, axis=0) - run
    group = -(-jnp.sum(run, axis=0) // EXPERT_ROWS) * EXPERT_ROWS
    ends = jnp.cumsum(group)
    sorted_off = (ends - group)[None, :] + before
    local_dest = jnp.sum(onehot * (local_off[:, None, :] + csum - 1), axis=2)
    runs = jnp.concatenate([local_off, sorted_off, run // RUN_ROWS], axis=1)[:, None, :]
    tile_start = jnp.arange(_sorted_rows(n, tile_rows) // EXPERT_ROWS, dtype=jnp.int32) * EXPERT_ROWS
    tile_expert = jnp.sum((tile_start[:, None] >= ends[None, :]).astype(jnp.int32), axis=1)
    return _RoutePlan(local_dest.astype(jnp.int32), runs.astype(jnp.int32),
                      jnp.minimum(tile_expert, N_EXPERTS - 1).astype(jnp.int32),
                      (ends[-1:] // EXPERT_ROWS).astype(jnp.int32))


def _start_runs(runs_ref, copy):
    for e in range(N_EXPERTS):
        local, dest, chunks = runs_ref[0, e], runs_ref[0, N_EXPERTS + e], runs_ref[0, 2 * N_EXPERTS + e]

        def start(j, carry, local=local, dest=dest):
            copy(pl.multiple_of(local + j * RUN_ROWS, RUN_ROWS),
                 pl.multiple_of(dest + j * RUN_ROWS, RUN_ROWS)).start()
            return carry

        lax.fori_loop(0, chunks, start, 0)


def _wait_runs(runs_ref, copy):
    total = 0
    for e in range(N_EXPERTS):
        total = total + runs_ref[0, 2 * N_EXPERTS + e]

    def wait(j, carry):
        copy(0, 0).wait()
        return carry

    lax.fori_loop(0, total, wait, 0)


def _dispatch_kernel(runs_ref, dest_ref, x_ref, xs_in_ref, xs_ref, buf_ref, sem):
    del xs_in_ref
    row = lax.broadcasted_iota(jnp.int32, (buf_ref.shape[0], x_ref.shape[0]), 0)
    pick = jnp.where(row == dest_ref[0:1, :], 1.0, jnp.where(row == dest_ref[1:2, :], 1.0, 0.0))
    buf_ref[...] = jnp.dot(pick.astype(BF16), x_ref[...].astype(BF16),
                           preferred_element_type=F32).astype(BF16)

    def copy(local, dest):
        return pltpu.make_async_copy(buf_ref.at[pl.ds(local, RUN_ROWS)], xs_ref.at[pl.ds(dest, RUN_ROWS)], sem)

    _start_runs(runs_ref, copy)
    _wait_runs(runs_ref, copy)


def _dispatch(x, plan, tile_rows, xs):
    n, d = x.shape
    n_tiles = n // tile_rows
    sorted_rows = xs.shape[0]
    return pl.pallas_call(
        _dispatch_kernel,
        grid=(n_tiles,),
        in_specs=[pl.BlockSpec((None, 1, 3 * N_EXPERTS), lambda i: (i, 0, 0), memory_space=pltpu.SMEM),
                  pl.BlockSpec((None, 2, tile_rows), lambda i: (i, 0, 0)),
                  pl.BlockSpec((tile_rows, d), lambda i: (i, 0)),
                  pl.BlockSpec(memory_space=pl.ANY)],
        out_specs=pl.BlockSpec(memory_space=pl.ANY),
        out_shape=jax.ShapeDtypeStruct((sorted_rows, d), BF16),
        scratch_shapes=[pltpu.VMEM((_compact_rows(tile_rows), d), BF16), pltpu.SemaphoreType.DMA(())],
        input_output_aliases={3: 0},
        name="moe_dispatch",
        compiler_params=_cparams(("arbitrary",)),
    )(plan.runs, plan.local_dest.reshape(n_tiles, 2, tile_rows), x, xs)


def _expert_kernel(te_ref, used_ref, xs_ref, wg_ref, wu_ref, wd_ref, o_ref, acc_ref):
    del te_ref
    t = pl.program_id(0)
    j = pl.program_id(1)
    last = pl.num_programs(1) - 1

    @pl.when(t < used_ref[0])
    def _():
        piece = _swiglu_piece(xs_ref[...], wg_ref[...], wu_ref[...], wd_ref[...])

        @pl.when(j == 0)
        def _():
            acc_ref[...] = piece

        @pl.when(jnp.logical_and(j > 0, j < last))
        def _():
            acc_ref[...] += piece

        @pl.when(j == last)
        def _():
            o_ref[...] = (acc_ref[...] + piece).astype(o_ref.dtype)

    @pl.when(jnp.logical_and(t >= used_ref[0], j == last))
    def _():
        o_ref[...] = jnp.zeros(o_ref.shape, o_ref.dtype)


def _experts(xs, plan, w_gu, w_down):
    r, d = xs.shape
    d_ff = w_down.shape[1]
    piece = d_ff // FFN_SPLIT
    layer = w_gu.layer
    assert FFN_SPLIT >= 2
    grid_spec = pltpu.PrefetchScalarGridSpec(
        num_scalar_prefetch=2,
        grid=(r // EXPERT_ROWS, FFN_SPLIT),
        in_specs=[pl.BlockSpec((EXPERT_ROWS, d), lambda t, j, te, used: (t, 0)),
                  pl.BlockSpec((None, None, d, piece), lambda t, j, te, used: (layer, te[t], 0, j)),
                  pl.BlockSpec((None, None, d, piece), lambda t, j, te, used: (layer, te[t], 0, FFN_SPLIT + j)),
                  pl.BlockSpec((None, None, piece, d), lambda t, j, te, used: (layer, te[t], j, 0))],
        out_specs=pl.BlockSpec((EXPERT_ROWS, d), lambda t, j, te, used: (t, 0)),
        scratch_shapes=[pltpu.VMEM((EXPERT_ROWS, d), F32)],
    )
    return pl.pallas_call(
        _expert_kernel,
        grid_spec=grid_spec,
        out_shape=jax.ShapeDtypeStruct((r, d), BF16),
        name="moe_experts",
        compiler_params=_cparams(("parallel", "arbitrary")),
    )(plan.tile_expert, plan.n_used, xs, w_gu.array, w_gu.array, w_down.array)


def _combine_kernel(runs_ref, next_runs_ref, x_ref, dest_ref, gate_ref, ys_ref, g_ref, b_ref, o_ref,
                    buf_ref, sem, *, alpha):
    i = pl.program_id(0)
    half = lax.rem(i, 2)

    def copy_into(which):
        def copy(local, src):
            return pltpu.make_async_copy(ys_ref.at[pl.ds(src, RUN_ROWS)],
                                         buf_ref.at[which, pl.ds(local, RUN_ROWS)], sem.at[which])
        return copy

    @pl.when(i == 0)
    def _():
        buf_ref[...] = jnp.zeros(buf_ref.shape, buf_ref.dtype)
        _start_runs(runs_ref, copy_into(0))

    @pl.when(i + 1 < pl.num_programs(0))
    def _():
        _start_runs(next_runs_ref, copy_into(1 - half))

    _wait_runs(runs_ref, copy_into(half))
    y = buf_ref[half]
    col = lax.broadcasted_iota(jnp.int32, (x_ref.shape[0], y.shape[0]), 1)
    gates = gate_ref[...]
    mixed = None
    for slot in range(2):
        pick = jnp.where(col == dest_ref[:, slot:slot + 1], 1.0, 0.0).astype(BF16)
        term = gates[:, slot:slot + 1] * jnp.dot(pick, y, preferred_element_type=F32)
        mixed = term if mixed is None else mixed + term
    o_ref[...] = _layer_norm(alpha * x_ref[...] + mixed, g_ref[...], b_ref[...])


def _combine(x, plan, gates_t, ys, ln_g, ln_b, *, tile_rows, alpha):
    n, d = x.shape
    n_tiles = n // tile_rows
    row_spec = pl.BlockSpec((tile_rows, d), lambda i: (i, 0))
    pair_spec = pl.BlockSpec((tile_rows, 2), lambda i: (i, 0))
    dest_cols = plan.local_dest.reshape(n_tiles, 2, tile_rows).transpose(0, 2, 1).reshape(n, 2)
    runs_shape = (None, 1, 3 * N_EXPERTS)
    return pl.pallas_call(
        functools.partial(_combine_kernel, alpha=alpha),
        grid=(n_tiles,),
        in_specs=[pl.BlockSpec(runs_shape, lambda i: (i, 0, 0), memory_space=pltpu.SMEM),
                  pl.BlockSpec(runs_shape, lambda i: (jnp.minimum(i + 1, n_tiles - 1), 0, 0),
                               memory_space=pltpu.SMEM),
                  row_spec, pair_spec, pair_spec,
                  pl.BlockSpec(memory_space=pl.ANY),
                  _resident(ln_g), _resident(ln_b)],
        out_specs=row_spec,
        out_shape=jax.ShapeDtypeStruct((n, d), F32),
        scratch_shapes=[pltpu.VMEM((2, _compact_rows(tile_rows), d), ys.dtype),
                        pltpu.SemaphoreType.DMA((2,))],
        name="moe_combine",
        compiler_params=_cparams(("arbitrary",)),
    )(plan.runs, plan.runs, x, dest_cols, gates_t, ys, ln_g, ln_b)


def _moe_layer(streams, w_router_t, w_gu, w_down, ln_g, ln_b, *, alpha):
    d = streams[0].shape[1]
    tile_rows = min([MOVE_ROWS] + [x.shape[0] for x in streams])
    assert all(x.shape[0] % tile_rows == 0 for x in streams)
    routed = [_router(x, w_router_t) for x in streams]
    plan = _route_plan(jnp.concatenate([idx for idx, _ in routed], axis=1), tile_rows)
    tile_lo, parts = 0, []
    for x in streams:
        tile_hi = tile_lo + x.shape[0] // tile_rows
        parts.append(plan._replace(local_dest=plan.local_dest[tile_lo:tile_hi], runs=plan.runs[tile_lo:tile_hi]))
        tile_lo = tile_hi
    xs = jnp.zeros((plan.tile_expert.shape[0] * EXPERT_ROWS, d), BF16)
    for x, part in zip(streams, parts):
        xs = _dispatch(x, part, tile_rows, xs)
    ys = _experts(xs, plan, w_gu, w_down)
    return [_combine(x, part, gates.T, ys, ln_g, ln_b, tile_rows=tile_rows, alpha=alpha)
            for x, part, (_, gates) in zip(streams, parts, routed)]


def _dot_nt(a, b):
    return lax.dot_general(a, b, (((1,), (1,)), ((), ())), preferred_element_type=F32)


def _kv_kernel(x_ref, w_ref, wvt_ref, k_ref, v_ref, kb_ref, vt_ref):
    qk_w = kb_ref.shape[-1]
    xb = x_ref[...].astype(BF16)
    kv = jnp.dot(xb, w_ref[...], preferred_element_type=F32)
    k = kv[:, :qk_w]
    v = kv[:, qk_w:]
    k_ref[...] = k
    for h in range(v_ref.shape[1]):
        v_ref[:, h, :] = v[:, h * v_ref.shape[2]:(h + 1) * v_ref.shape[2]]
    kb_ref[...] = k.astype(BF16)
    vt_ref[...] = _dot_nt(wvt_ref[...], xb).astype(BF16)


def _kv_proj(x, w_kv, w_vt):
    bsz, t, d = x.shape
    qk_w = w_kv.shape[1] // 2
    rows = min(ATT_K, t)
    assert t % rows == 0
    v_heads = (qk_w // (2 * HEAD_DIM), 2 * HEAD_DIM)
    row_spec = pl.BlockSpec((None, rows, qk_w), lambda b, i: (b, i, 0))
    return pl.pallas_call(
        _kv_kernel,
        grid=(bsz, t // rows),
        in_specs=[pl.BlockSpec((None, rows, d), lambda b, i: (b, i, 0)),
                  _resident(w_kv), _resident(w_vt)],
        out_specs=[row_spec,
                   pl.BlockSpec((None, rows) + v_heads, lambda b, i: (b, i, 0, 0)),
                   row_spec,
                   pl.BlockSpec((None, None, qk_w, rows), lambda b, i: (b, i, 0, 0))],
        out_shape=[jax.ShapeDtypeStruct((bsz, t, qk_w), F32),
                   jax.ShapeDtypeStruct((bsz, t) + v_heads, F32),
                   jax.ShapeDtypeStruct((bsz, t, qk_w), BF16),
                   jax.ShapeDtypeStruct((bsz, t // rows, qk_w, rows), BF16)],
        name="kv_proj",
        compiler_params=_cparams(("parallel", "parallel")),
    )(x, w_kv, w_vt)


def _t5_bucket(rel):
    nb = NUM_BUCKETS // 2
    max_exact = nb // 2
    ret = (rel > 0).astype(jnp.int32) * nb
    n = jnp.abs(rel)
    nf = jnp.maximum(n, 1).astype(jnp.float32)
    large = max_exact + (jnp.log(nf / max_exact) / math.log(MAX_DISTANCE / max_exact)
                         * (nb - max_exact)).astype(jnp.int32)
    large = jnp.minimum(large, nb - 1)
    return ret + jnp.where(n < max_exact, n, large)


def _bucket_table(q_pos, k_pos, visible=None):
    bucket = _t5_bucket(k_pos[:, None] - q_pos[None, :])
    return bucket if visible is None else jnp.where(visible, bucket, -1)


def _bias_kernel(far_ref, rb_ref, bucket_ref, o_ref):
    h = pl.program_id(0)
    bucket = bucket_ref[...]
    far = rb_ref[far_ref[0], h]
    out = jnp.zeros(bucket.shape, F32)
    for b in range(NUM_BUCKETS):
        out = jnp.where(bucket == b, rb_ref[b, h] - far, out)
    o_ref[...] = jnp.where(bucket < 0, NEG_INF, out * LOG2_E)


def _bias_tables(rel_bias, buckets):
    n_heads2 = rel_bias.shape[1]
    far_bucket = _t5_bucket(jnp.full((1,), -MAX_DISTANCE, jnp.int32))
    return pl.pallas_call(
        _bias_kernel,
        grid=(n_heads2,),
        in_specs=[pl.BlockSpec(memory_space=pltpu.SMEM), pl.BlockSpec(memory_space=pltpu.SMEM),
                  _resident(buckets)],
        out_specs=pl.BlockSpec((None,) + buckets.shape, lambda h: (h, 0, 0)),
        out_shape=jax.ShapeDtypeStruct((n_heads2,) + buckets.shape, F32),
        name="bias_tables",
        compiler_params=_cparams(("parallel",)),
    )(far_bucket, rel_bias.astype(F32), buckets)


def _att_init(x_ref, wqt_ref, qa_ref, qb_ref, m_ref, l_ref, acc_ref):
    qt = _dot_nt(wqt_ref[...], x_ref[...].astype(BF16)) * (HEAD_DIM ** -0.5 * LOG2_E)
    row = lax.broadcasted_iota(jnp.int32, (LANES, qt.shape[1]), 0)
    for h in range(qa_ref.shape[0]):
        slab = qt[h * LANES:(h + 1) * LANES, :]
        qa_ref[h] = jnp.where(row < HEAD_DIM, slab, 0.0).astype(BF16)
        qb_ref[h] = jnp.where(row >= HEAD_DIM, slab, 0.0).astype(BF16)
    m_ref[...] = jnp.full(m_ref.shape, NEG_INF, F32)
    l_ref[...] = jnp.zeros(l_ref.shape, F32)
    acc_ref[...] = jnp.zeros(acc_ref.shape, F32)


def _att_block(qa_ref, qb_ref, k, vt_slab, bias, m_ref, l_ref, acc_ref):
    n_heads = 2 * qa_ref.shape[0]

    def scores(hh):
        h = hh // 2
        q_ref = qa_ref if hh % 2 == 0 else qb_ref
        s = jnp.dot(k[:, h * LANES:(h + 1) * LANES], q_ref[h], preferred_element_type=F32)
        return s if bias is None else s + bias(hh)

    s_of, p_of, scale_of = {}, {}, {}

    def softmax(hh):
        s = s_of.pop(hh)
        m_old = m_ref[hh]
        m_new = jnp.maximum(m_old, jnp.max(s, axis=0, keepdims=True))
        scale = jnp.exp2(m_old - m_new)
        p = jnp.exp2(s - m_new)
        l_ref[hh] = scale * l_ref[hh] + jnp.sum(p, axis=0, keepdims=True)
        m_ref[hh] = m_new
        p_of[hh] = p.astype(BF16)
        scale_of[hh] = scale

    def accumulate(hh):
        acc_ref[hh] = scale_of.pop(hh) * acc_ref[hh] + jnp.dot(vt_slab(hh // 2), p_of.pop(hh),
                                                               preferred_element_type=F32)

    n_groups = n_heads // HEAD_GROUP
    group = lambda t: range(t * HEAD_GROUP, (t + 1) * HEAD_GROUP)
    for t in range(-2, n_groups):
        if t + 2 < n_groups:
            for hh in group(t + 2):
                s_of[hh] = scores(hh)
        if 0 <= t + 1 < n_groups:
            for hh in group(t + 1):
                softmax(hh)
        if t >= 0:
            for hh in group(t):
                accumulate(hh)


def _att_finish(x_ref, lam_ref, sg_ref, wout_ref, g_ref, b_ref, o_ref, l_ref, acc_ref, heads_ref,
                *, lam_init, alpha):
    lp = lam_ref[...]
    lam = (jnp.exp(jnp.sum(lp[0:1] * lp[1:2], axis=-1, keepdims=True))
           - jnp.exp(jnp.sum(lp[2:3] * lp[3:4], axis=-1, keepdims=True)) + lam_init)
    for h in range(acc_ref.shape[0] // 2):
        a = acc_ref[2 * h] * (1.0 / l_ref[2 * h]) - (lam / l_ref[2 * h + 1]) * acc_ref[2 * h + 1]
        r = a * lax.rsqrt(jnp.mean(a * a, axis=0, keepdims=True) + LN_EPS) * sg_ref[...]
        heads_ref[h * LANES:(h + 1) * LANES, :] = r * (1.0 - lam_init)
    y = lax.dot_general(heads_ref[...].astype(BF16), wout_ref[...], (((0,), (0,)), ((), ())),
                        preferred_element_type=F32)
    o_ref[...] = _layer_norm(alpha * x_ref[...] + y, g_ref[...], b_ref[...])


def _att_scratch(q_rows, width):
    n_slabs = width // LANES
    return [pltpu.VMEM((n_slabs, LANES, q_rows), BF16), pltpu.VMEM((n_slabs, LANES, q_rows), BF16),
            pltpu.VMEM((2 * n_slabs, 1, q_rows), F32), pltpu.VMEM((2 * n_slabs, 1, q_rows), F32),
            pltpu.VMEM((2 * n_slabs, LANES, q_rows), F32), pltpu.VMEM((width, q_rows), F32)]


STEP_FAR_WIDE = 0
STEP_FAR_ONE = 1
STEP_NEAR = 2
FAR_TILES = 4


def _prompt_att_kernel(qi_ref, kind_ref, first_ref, wide_ref, ka_ref_idx, kb_ref_idx,
                       x_ref, kw_ref, vtw_ref, ka_ref, vta_ref, kb_ref, vtb_ref, wqt_ref, bias_ref, lam_ref,
                       sg_ref, wout_ref, g_ref, b_ref, o_ref, qa_ref, qb_ref, m_ref, l_ref, acc_ref,
                       heads_ref, *, lam_init, alpha):
    del wide_ref, ka_ref_idx, kb_ref_idx
    p = pl.program_id(1)
    kind = kind_ref[p]
    tk = ka_ref.shape[0]
    state = (m_ref, l_ref, acc_ref)

    def slab(ref, h):
        return ref[h * LANES:(h + 1) * LANES, :]

    @pl.when(first_ref[p] == 1)
    def _():
        _att_init(x_ref, wqt_ref, qa_ref, qb_ref, *state)

    @pl.when(kind == STEP_FAR_WIDE)
    def _():
        def vt_slab(h):
            return jnp.concatenate([vtw_ref[c, h * LANES:(h + 1) * LANES, :]
                                    for c in range(vtw_ref.shape[0])], axis=1)
        _att_block(qa_ref, qb_ref, kw_ref[...], vt_slab, None, *state)

    @pl.when(kind == STEP_FAR_ONE)
    def _():
        _att_block(qa_ref, qb_ref, ka_ref[...], lambda h: slab(vta_ref, h), None, *state)

    @pl.when(kind == STEP_NEAR)
    def _():
        hide = jnp.where(qi_ref[p] == 0, NEG_INF, 0.0)

        def bias(hh):
            return jnp.concatenate([bias_ref[hh, 0:tk, :] + hide, bias_ref[hh, tk:2 * tk, :]], axis=0)

        def vt_slab(h):
            return jnp.concatenate([slab(vta_ref, h), slab(vtb_ref, h)], axis=1)

        k = jnp.concatenate([ka_ref[...], kb_ref[...]], axis=0)
        _att_block(qa_ref, qb_ref, k, vt_slab, bias, *state)
        _att_finish(x_ref, lam_ref, sg_ref, wout_ref, g_ref, b_ref, o_ref, l_ref, acc_ref, heads_ref,
                    lam_init=lam_init, alpha=alpha)


def _prompt_bias(rel_bias, tq):
    pos = jnp.arange(tq, dtype=jnp.int32)
    visible = pos[:, None] < (pos[None, :] // CHUNK + 1) * CHUNK
    buckets = jnp.concatenate([_bucket_table(pos + tq, pos), _bucket_table(pos, pos, visible)], axis=0)
    return _bias_tables(rel_bias, buckets)


def _prompt_steps(n_q):
    steps = []
    wide = ka = kb = 0
    for i in range(n_q):
        n_far = max(i - 1, 0)
        first = 1
        for c in range(n_far // FAR_TILES):
            wide = c
            steps.append((i, STEP_FAR_WIDE, first, wide, ka, kb))
            first = 0
        for j in range(n_far - n_far % FAR_TILES, n_far):
            ka = j
            steps.append((i, STEP_FAR_ONE, first, wide, ka, kb))
            first = 0
        ka, kb = max(i - 1, 0), i
        steps.append((i, STEP_NEAR, first, wide, ka, kb))
    return [jnp.array(col, jnp.int32) for col in zip(*steps)]


def _prompt_attention(x, kb, vt, bias, w_qt, lam_p, subln_g, w_out, ln_g, ln_b, *, lam_init, alpha):
    bsz, s_len, d = x.shape
    width = kb.shape[-1]
    tq = vt.shape[-1]
    n_q = s_len // tq
    assert s_len % tq == 0 and tq % CHUNK == 0 and tq >= MAX_DISTANCE
    assert n_q % FAR_TILES == 0 or n_q <= FAR_TILES
    steps = _prompt_steps(n_q)
    wide_tiles = min(FAR_TILES, n_q)
    x_spec = pl.BlockSpec((None, tq, d), lambda b, p, qi, kind, first, wide, ka, kb: (b, qi[p], 0))
    grid_spec = pltpu.PrefetchScalarGridSpec(
        num_scalar_prefetch=len(steps),
        grid=(bsz, steps[0].shape[0]),
        in_specs=[x_spec,
                  pl.BlockSpec((None, wide_tiles * tq, width),
                               lambda b, p, qi, kind, first, wide, ka, kb: (b, wide[p], 0)),
                  pl.BlockSpec((None, wide_tiles, width, tq),
                               lambda b, p, qi, kind, first, wide, ka, kb: (b, wide[p], 0, 0)),
                  pl.BlockSpec((None, tq, width), lambda b, p, qi, kind, first, wide, ka, kb: (b, ka[p], 0)),
                  pl.BlockSpec((None, None, width, tq),
                               lambda b, p, qi, kind, first, wide, ka, kb: (b, ka[p], 0, 0)),
                  pl.BlockSpec((None, tq, width), lambda b, p, qi, kind, first, wide, ka, kb: (b, kb[p], 0)),
                  pl.BlockSpec((None, None, width, tq),
                               lambda b, p, qi, kind, first, wide, ka, kb: (b, kb[p], 0, 0)),
                  _resident(w_qt), _resident(bias), _resident(lam_p), _resident(subln_g), _resident(w_out),
                  _resident(ln_g), _resident(ln_b)],
        out_specs=x_spec,
        scratch_shapes=_att_scratch(tq, width),
    )
    return pl.pallas_call(
        functools.partial(_prompt_att_kernel, lam_init=lam_init, alpha=alpha),
        grid_spec=grid_spec,
        out_shape=jax.ShapeDtypeStruct((bsz, s_len, d), F32),
        name="prompt_attention",
        compiler_params=_cparams(("parallel", "arbitrary")),
    )(*steps, x, kb, vt, kb, vt, kb, vt, _operand(w_qt), bias, lam_p, subln_g, _operand(w_out), ln_g, ln_b)


def _sample_att_kernel(x_ref, ck_ref, cv_ref, kn_ref, vtn_ref, wqt_ref, bias_ref, lam_ref, sg_ref,
                       wout_ref, g_ref, b_ref, o_ref, qa_ref, qb_ref, m_ref, l_ref, acc_ref, heads_ref,
                       *, lam_init, alpha):
    j = pl.program_id(1)
    n_cache = pl.num_programs(1) - 1
    tk = ck_ref.shape[0]
    t_new = kn_ref.shape[0]
    state = (m_ref, l_ref, acc_ref)

    def near_bias(hh):
        return bias_ref[hh, 0:tk, :]

    def new_bias(hh):
        return bias_ref[hh, tk:tk + t_new, :]

    def cached_vt_slab(h):
        return jnp.transpose(cv_ref[:, h * LANES:(h + 1) * LANES]).astype(BF16)

    def new_vt_slab(h):
        return vtn_ref[h * LANES:(h + 1) * LANES, :]

    @pl.when(j == 0)
    def _():
        _att_init(x_ref, wqt_ref, qa_ref, qb_ref, *state)

    @pl.when(j < n_cache - 1)
    def _():
        _att_block(qa_ref, qb_ref, ck_ref[...].astype(BF16), cached_vt_slab, None, *state)

    @pl.when(j == n_cache - 1)
    def _():
        _att_block(qa_ref, qb_ref, ck_ref[...].astype(BF16), cached_vt_slab, near_bias, *state)

    @pl.when(j == n_cache)
    def _():
        _att_block(qa_ref, qb_ref, kn_ref[...], new_vt_slab, new_bias, *state)
        _att_finish(x_ref, lam_ref, sg_ref, wout_ref, g_ref, b_ref, o_ref, l_ref, acc_ref, heads_ref,
                    lam_init=lam_init, alpha=alpha)


def _sample_bias(rel_bias, past, t):
    tk = min(CACHE_K, past)
    q_pos = past + jnp.arange(t, dtype=jnp.int32)
    buckets = jnp.concatenate([_bucket_table(q_pos, past - tk + jnp.arange(tk, dtype=jnp.int32)),
                               _bucket_table(q_pos, q_pos)], axis=0)
    return _bias_tables(rel_bias, buckets)


def _sample_attention(x, cache_k, cache_v, kb, vt, bias, w_qt, lam_p, subln_g, w_out, ln_g, ln_b,
                      *, lam_init, alpha):
    bsz, t, d = x.shape
    past = cache_k.shape[1]
    width = kb.shape[-1]
    tk = min(CACHE_K, past)
    assert past % tk == 0 and tk >= MAX_DISTANCE and vt.shape[1] == 1
    n_cache = past // tk
    cache_spec = pl.BlockSpec((None, tk, width), lambda b, j: (b, jnp.minimum(j, n_cache - 1), 0))
    x_spec = pl.BlockSpec((None, t, d), lambda b, j: (b, 0, 0))
    return pl.pallas_call(
        functools.partial(_sample_att_kernel, lam_init=lam_init, alpha=alpha),
        grid=(bsz, n_cache + 1),
        in_specs=[x_spec, cache_spec, cache_spec,
                  pl.BlockSpec((None, t, width), lambda b, j: (b, 0, 0)),
                  pl.BlockSpec((None, None, width, t), lambda b, j: (b, 0, 0, 0)),
                  _resident(w_qt), _resident(bias), _resident(lam_p), _resident(subln_g), _resident(w_out),
                  _resident(ln_g), _resident(ln_b)],
        out_specs=x_spec,
        out_shape=jax.ShapeDtypeStruct((bsz, t, d), F32),
        scratch_shapes=_att_scratch(t, width),
        name="sample_attention",
        compiler_params=_cparams(("parallel", "arbitrary")),
    )(x, cache_k, cache_v, kb, vt, _operand(w_qt), bias, lam_p, subln_g, _operand(w_out), ln_g, ln_b)


class _Stream:
    def __init__(self, x, cache_k, cache_v, emit_v):
        self.bsz, self.t, self.d = x.shape
        self.h = x.reshape(self.bsz * self.t, self.d)
        self.cache_k, self.cache_v, self.emit_v = cache_k, cache_v, emit_v
        self.sgu_rows = []
        self.k_new = self.v_new = self.kb = self.vt = self.bias = None


def _mixer(s, i, w, *, depth, alpha):
    n_a = depth // 2
    if i < n_a:
        chunk = min(s.t, SGU_CHUNK)
        s.h, v_rows = _sgu_layer(
            s.h, _Stacked(w["a_w_in"], i), w["a_ln_g"][i], w["a_ln_b"][i], w["a_w_s"][i][:, :chunk, :chunk],
            w["a_b_s"][i][:, :chunk].T, _Stacked(w["a_w_out"], i), w["ln_mix_g"][i], w["ln_mix_b"][i],
            chunk=chunk, alpha=alpha, emit_v=s.emit_v)
        s.sgu_rows.append(v_rows)
        return
    h3 = s.h.reshape(s.bsz, s.t, s.d)
    if s.k_new is None:
        s.k_new, s.v_new, s.kb, s.vt = _kv_proj(h3, w["w_kv"], w["w_vt"])
        if s.cache_k is None:
            s.bias = _prompt_bias(w["rel_bias"], s.vt.shape[-1])
        else:
            s.bias = _sample_bias(w["rel_bias"], s.cache_k.shape[1], s.t)
    j = i - n_a
    lam_init = 0.8 - 0.6 * math.exp(-0.3 * i)
    args = (s.kb, s.vt, s.bias, _Stacked(w["b_w_qt"], j), w["b_lam"][j], w["b_subln_g"][j],
            _Stacked(w["b_w_out"], j), w["ln_mix_g"][i], w["ln_mix_b"][i])
    if s.cache_k is None:
        out = _prompt_attention(h3, *args, lam_init=lam_init, alpha=alpha)
    else:
        past = s.cache_k.shape[1]
        out = _sample_attention(h3, s.cache_k.reshape(s.bsz, past, -1), s.cache_v.reshape(s.bsz, past, -1),
                                *args, lam_init=lam_init, alpha=alpha)
    s.h = out.reshape(s.bsz * s.t, s.d)


def _run_layers(streams, w, *, depth):
    alpha = (2 * depth) ** 0.25
    for i in range(depth):
        for s in streams:
            _mixer(s, i, w, depth=depth, alpha=alpha)
        if i % 2 == 0:
            for s in streams:
                s.h = _ffn_layer(s.h, _Stacked(w["ffn_w_gu"], i // 2), _Stacked(w["ffn_w_down"], i // 2),
                                 w["ln_ffn_g"][i], w["ln_ffn_b"][i], alpha=alpha)
        else:
            outs = _moe_layer([s.h for s in streams], w["moe_w_router_t"][i // 2],
                              _Stacked(w["moe_w_gu"], i // 2), _Stacked(w["moe_w_down"], i // 2),
                              w["ln_ffn_g"][i], w["ln_ffn_b"][i], alpha=alpha)
            for s, out in zip(streams, outs):
                s.h = out


def kernel(x_prompt, x_sample, cache_k, cache_v, a_w_in, a_ln_g, a_ln_b, a_w_s, a_b_s, a_w_out, w_kv, b_w_q,
           b_lam, b_subln_g, b_w_out, rel_bias, ln_mix_g, ln_mix_b, ln_ffn_g, ln_ffn_b, ffn_w_gu, ffn_w_down,
           moe_w_router, moe_w_gu, moe_w_down):
    depth = ln_mix_g.shape[0]
    row = lambda a: a[:, None, :]
    w = dict(
        a_w_in=a_w_in.astype(BF16), a_ln_g=row(a_ln_g), a_ln_b=row(a_ln_b), a_w_s=a_w_s, a_b_s=a_b_s,
        a_w_out=a_w_out.astype(BF16), w_kv=w_kv.astype(BF16),
        w_vt=jnp.transpose(w_kv[:, w_kv.shape[1] // 2:]).astype(BF16),
        b_w_qt=jnp.swapaxes(b_w_q, 1, 2).astype(BF16), b_lam=b_lam,
        b_subln_g=b_subln_g[:, :, None], b_w_out=b_w_out.astype(BF16), rel_bias=rel_bias,
        ln_mix_g=row(ln_mix_g), ln_mix_b=row(ln_mix_b), ln_ffn_g=row(ln_ffn_g), ln_ffn_b=row(ln_ffn_b),
        ffn_w_gu=ffn_w_gu.astype(BF16), ffn_w_down=ffn_w_down.astype(BF16),
        moe_w_router_t=jnp.swapaxes(moe_w_router, 1, 2), moe_w_gu=moe_w_gu.astype(BF16),
        moe_w_down=moe_w_down.astype(BF16))
    prompt = _Stream(x_prompt, None, None, emit_v=False)
    sample = _Stream(x_sample, cache_k, cache_v, emit_v=True)
    _run_layers([prompt, sample], w, depth=depth)
    d_sgu = a_w_out.shape[1]
    k_heads = (rel_bias.shape[1], HEAD_DIM)
    return (prompt.h.reshape(x_prompt.shape), sample.h.reshape(x_sample.shape),
            prompt.k_new.reshape(x_prompt.shape[:2] + k_heads), prompt.v_new,
            sample.k_new.reshape(x_sample.shape[:2] + k_heads), sample.v_new,
            jnp.stack([r.reshape(x_sample.shape[:2] + (d_sgu,)) for r in sample.sgu_rows], axis=0))
```

```python
import functools
import math
from typing import NamedTuple

import jax
import jax.numpy as jnp
from jax import lax
from jax.experimental import pallas as pl
from jax.experimental.pallas import tpu as pltpu

F32 = jnp.float32
BF16 = jnp.bfloat16

CHUNK = 64
SGU_CHUNK = 128
SGU_GROUPS = 4
HEAD_DIM = 64
NUM_BUCKETS = 32
MAX_DISTANCE = 128
N_EXPERTS = 8
LN_EPS = 1e-5
NEG_INF = -1e30
SQRT_HALF = math.sqrt(0.5)
LOG2_E = math.log2(math.e)

LANES = 128
VMEM_LIMIT_BYTES = 56 * 1024 * 1024

SGU_ROWS = 256
FFN_ROWS = 512
FFN_SPLIT = 2
ROUTE_ROWS = 1024
MOVE_ROWS = 512
RUN_ROWS = 16
EXPERT_ROWS = 512
ATT_K = 256
CACHE_K = 512
HEAD_GROUP = 2


def _cparams(semantics):
    return pltpu.CompilerParams(dimension_semantics=semantics, vmem_limit_bytes=VMEM_LIMIT_BYTES)


class _Stacked(NamedTuple):
    array: jax.Array
    layer: int

    @property
    def shape(self):
        return self.array.shape[1:]


def _operand(a):
    return a.array if isinstance(a, _Stacked) else a


def _resident(a):
    nd = len(a.shape)
    if isinstance(a, _Stacked):
        return pl.BlockSpec((None,) + a.shape, lambda *_: (a.layer,) + (0,) * nd,
                            pipeline_mode=pl.Buffered(1))
    return pl.BlockSpec(a.shape, lambda *_: (0,) * nd, pipeline_mode=pl.Buffered(1))


def _layer_norm(z, g, b):
    mu = jnp.mean(z, axis=-1, keepdims=True)
    zc = z - mu
    var = jnp.mean(zc * zc, axis=-1, keepdims=True)
    return zc * lax.rsqrt(var + LN_EPS) * g + b


def _dot_nt(a, b):
    return lax.dot_general(a, b, (((1,), (1,)), ((), ())), preferred_element_type=F32)


def _gelu_exact(h):
    return 0.5 * h * (1.0 + lax.erf(h * SQRT_HALF))


def _sgu_kernel(x_ref, win_ref, lg_ref, lb_ref, ws_ref, bs_ref, wout_ref, mg_ref, mb_ref,
                o_ref, *rest, chunk, alpha, emit_v):
    if emit_v:
        v_ref, gate_ref = rest
    else:
        (gate_ref,) = rest
    rows = x_ref.shape[0]
    d_sgu = wout_ref.shape[0]
    gw = d_sgu // SGU_GROUPS
    x = x_ref[...]
    xb = x.astype(BF16)
    v = _gelu_exact(jnp.dot(xb, win_ref[:, d_sgu:], preferred_element_type=F32))
    vn = _layer_norm(v, lg_ref[...], lb_ref[...])
    if emit_v:
        v_ref[...] = vn
    vb = vn.astype(BF16)
    u = _gelu_exact(jnp.dot(xb, win_ref[:, :d_sgu], preferred_element_type=F32))
    tril = (lax.broadcasted_iota(jnp.int32, (chunk, chunk), 0)
            >= lax.broadcasted_iota(jnp.int32, (chunk, chunk), 1))
    for g in range(SGU_GROUPS):
        wg = jnp.where(tril, ws_ref[g], 0.0).astype(BF16)
        bcol = bs_ref[:, g:g + 1]
        for c in range(rows // chunk):
            r0, r1, c0, c1 = c * chunk, (c + 1) * chunk, g * gw, (g + 1) * gw
            mixed = jnp.dot(wg, vb[r0:r1, c0:c1], preferred_element_type=F32) + bcol
            gate_ref[r0:r1, c0:c1] = (u[r0:r1, c0:c1] * mixed).astype(BF16)
    y = jnp.dot(gate_ref[...], wout_ref[...], preferred_element_type=F32)
    o_ref[...] = _layer_norm(alpha * x + y, mg_ref[...], mb_ref[...])


def _sgu_layer(x, w_in, ln_g, ln_b, w_s, b_s, w_out, mix_g, mix_b, *, chunk, alpha, emit_v):
    n, d = x.shape
    d_sgu = w_out.shape[0]
    rows = min(SGU_ROWS, n)
    assert n % rows == 0 and rows % chunk == 0
    row_spec = pl.BlockSpec((rows, d), lambda i: (i, 0))
    out_shape = [jax.ShapeDtypeStruct((n, d), F32)]
    out_specs = [row_spec]
    if emit_v:
        out_shape.append(jax.ShapeDtypeStruct((n, d_sgu), F32))
        out_specs.append(pl.BlockSpec((rows, d_sgu), lambda i: (i, 0)))
    res = pl.pallas_call(
        functools.partial(_sgu_kernel, chunk=chunk, alpha=alpha, emit_v=emit_v),
        grid=(n // rows,),
        in_specs=[row_spec] + [_resident(a) for a in (w_in, ln_g, ln_b, w_s, b_s, w_out, mix_g, mix_b)],
        out_specs=out_specs,
        out_shape=out_shape,
        scratch_shapes=[pltpu.VMEM((rows, d_sgu), BF16)],
        name="sgu_layer",
        compiler_params=_cparams(("parallel",)),
    )(x, _operand(w_in), ln_g, ln_b, w_s, b_s, _operand(w_out), mix_g, mix_b)
    return res if emit_v else (res[0], None)


def _swiglu_piece(xb, wg, wu, wd):
    g = jnp.dot(xb, wg, preferred_element_type=F32)
    u = jnp.dot(xb, wu, preferred_element_type=F32)
    a = (g * jax.nn.sigmoid(g) * u).astype(BF16)
    return jnp.dot(a, wd, preferred_element_type=F32)


def _ffn_kernel(x_ref, wgu_ref, wd_ref, g_ref, b_ref, o_ref, *, alpha):
    d_ff = wd_ref.shape[0]
    piece = d_ff // FFN_SPLIT
    x = x_ref[...]
    xb = x.astype(BF16)
    acc = None
    for j in range(FFN_SPLIT):
        t = _swiglu_piece(xb, wgu_ref[:, j * piece:(j + 1) * piece],
                          wgu_ref[:, d_ff + j * piece:d_ff + (j + 1) * piece],
                          wd_ref[j * piece:(j + 1) * piece, :])
        acc = t if acc is None else acc + t
    o_ref[...] = _layer_norm(alpha * x + acc, g_ref[...], b_ref[...])


def _ffn_layer(x, w_gu, w_down, ln_g, ln_b, *, alpha):
    n, d = x.shape
    rows = min(FFN_ROWS, n)
    assert n % rows == 0 and w_down.shape[0] % (FFN_SPLIT * LANES) == 0
    row_spec = pl.BlockSpec((rows, d), lambda i: (i, 0))
    return pl.pallas_call(
        functools.partial(_ffn_kernel, alpha=alpha),
        grid=(n // rows,),
        in_specs=[row_spec] + [_resident(a) for a in (w_gu, w_down, ln_g, ln_b)],
        out_specs=row_spec,
        out_shape=jax.ShapeDtypeStruct((n, d), F32),
        name="dense_ffn",
        compiler_params=_cparams(("parallel",)),
    )(x, _operand(w_gu), _operand(w_down), ln_g, ln_b)


def _router_kernel(x_ref, wr_ref, idx_ref, gate_ref):
    logits = _dot_nt(wr_ref[...].astype(BF16), x_ref[...].astype(BF16))
    ids = lax.broadcasted_iota(jnp.int32, logits.shape, 0)
    m1 = jnp.max(logits, axis=0, keepdims=True)
    i1 = jnp.min(jnp.where(logits == m1, ids, N_EXPERTS), axis=0, keepdims=True)
    rest = jnp.where(ids == i1, -jnp.inf, logits)
    m2 = jnp.max(rest, axis=0, keepdims=True)
    i2 = jnp.min(jnp.where(rest == m2, ids, N_EXPERTS), axis=0, keepdims=True)
    e2 = jnp.exp(m2 - m1)
    den = 1.0 + e2
    idx_ref[...] = jnp.concatenate([i1, i2], axis=0)
    gate_ref[...] = jnp.concatenate([1.0 / den, e2 / den], axis=0)


def _router(x, w_router_t):
    n, d = x.shape
    rows = min(ROUTE_ROWS, n)
    assert n % rows == 0
    return pl.pallas_call(
        _router_kernel,
        grid=(n // rows,),
        in_specs=[pl.BlockSpec((rows, d), lambda i: (i, 0)), _resident(w_router_t)],
        out_specs=[pl.BlockSpec((2, rows), lambda i: (0, i)), pl.BlockSpec((2, rows), lambda i: (0, i))],
        out_shape=[jax.ShapeDtypeStruct((2, n), jnp.int32), jax.ShapeDtypeStruct((2, n), F32)],
        name="moe_router",
        compiler_params=_cparams(("parallel",)),
    )(x, w_router_t)


class _RoutePlan(NamedTuple):
    local_dest: jax.Array
    runs: jax.Array
    tile_expert: jax.Array
    n_used: jax.Array


def _compact_rows(tile_rows):
    bound = 2 * tile_rows + N_EXPERTS * (RUN_ROWS - 1)
    return -(-bound // LANES) * LANES


def _sorted_rows(n_tokens, tile_rows):
    bound = 2 * n_tokens + (n_tokens // tile_rows) * N_EXPERTS * (RUN_ROWS - 1)
    return -(-bound // EXPERT_ROWS) * EXPERT_ROWS + N_EXPERTS * EXPERT_ROWS


def _route_plan(idx, tile_rows):
    n = idx.shape[1]
    n_tiles = n // tile_rows
    experts = idx.reshape(2, n_tiles, tile_rows).transpose(1, 0, 2).reshape(n_tiles, 2 * tile_rows)
    onehot = (experts[:, :, None] == jnp.arange(N_EXPERTS, dtype=jnp.int32)).astype(jnp.int32)
    order = jnp.arange(2 * tile_rows, dtype=jnp.int32)
    tri = (order[:, None] >= order[None, :]).astype(BF16)
    csum = jnp.einsum("uv,ive->iue", tri, onehot.astype(BF16),
                      preferred_element_type=F32).astype(jnp.int32)
    count = csum[:, -1, :]
    run = -(-count // RUN_ROWS) * RUN_ROWS
    local_off = jnp.cumsum(run, axis=1) - run
    before = jnp.cumsum(run, axis=0) - run
    group = -(-jnp.sum(run, axis=0) // EXPERT_ROWS) * EXPERT_ROWS
    ends = jnp.cumsum(group)
    sorted_off = (ends - group)[None, :] + before
    local_dest = jnp.sum(onehot * (local_off[:, None, :] + csum - 1), axis=2)
    runs = jnp.concatenate([local_off, sorted_off, run // RUN_ROWS], axis=1)[:, None, :]
    tile_start = jnp.arange(_sorted_rows(n, tile_rows) // EXPERT_ROWS, dtype=jnp.int32) * EXPERT_ROWS
    tile_expert = jnp.sum((tile_start[:, None] >= ends[None, :]).astype(jnp.int32), axis=1)
    return _RoutePlan(local_dest.astype(jnp.int32), runs.astype(jnp.int32),
                      jnp.minimum(tile_expert, N_EXPERTS - 1).astype(jnp.int32),
                      (ends[-1:] // EXPERT_ROWS).astype(jnp.int32))


def _start_runs(runs_ref, copy):
    for e in range(N_EXPERTS):
        local, dest, chunks = runs_ref[0, e], runs_ref[0, N_EXPERTS + e], runs_ref[0, 2 * N_EXPERTS + e]

        def start(j, carry, local=local, dest=dest):
            copy(pl.multiple_of(local + j * RUN_ROWS, RUN_ROWS),
                 pl.multiple_of(dest + j * RUN_ROWS, RUN_ROWS)).start()
            return carry

        lax.fori_loop(0, chunks, start, 0)


def _wait_runs(runs_ref, copy):
    total = 0
    for e in range(N_EXPERTS):
        total = total + runs_ref[0, 2 * N_EXPERTS + e]

    def wait(j, carry):
        copy(0, 0).wait()
        return carry

    lax.fori_loop(0, total, wait, 0)


def _dispatch_kernel(runs_ref, dest_ref, x_ref, xs_in_ref, xs_ref, buf_ref, sem):
    del xs_in_ref
    row = lax.broadcasted_iota(jnp.int32, (buf_ref.shape[0], x_ref.shape[0]), 0)
    pick = jnp.where(row == dest_ref[0:1, :], 1.0, jnp.where(row == dest_ref[1:2, :], 1.0, 0.0))
    buf_ref[...] = jnp.dot(pick.astype(BF16), x_ref[...].astype(BF16),
                           preferred_element_type=F32).astype(BF16)

    def copy(local, dest):
        return pltpu.make_async_copy(buf_ref.at[pl.ds(local, RUN_ROWS)], xs_ref.at[pl.ds(dest, RUN_ROWS)], sem)

    _start_runs(runs_ref, copy)
    _wait_runs(runs_ref, copy)


def _dispatch(x, plan, tile_rows, xs):
    n, d = x.shape
    n_tiles = n // tile_rows
    sorted_rows = xs.shape[0]
    return pl.pallas_call(
        _dispatch_kernel,
        grid=(n_tiles,),
        in_specs=[pl.BlockSpec((None, 1, 3 * N_EXPERTS), lambda i: (i, 0, 0), memory_space=pltpu.SMEM),
                  pl.BlockSpec((None, 2, tile_rows), lambda i: (i, 0, 0)),
                  pl.BlockSpec((tile_rows, d), lambda i: (i, 0)),
                  pl.BlockSpec(memory_space=pl.ANY)],
        out_specs=pl.BlockSpec(memory_space=pl.ANY),
        out_shape=jax.ShapeDtypeStruct((sorted_rows, d), BF16),
        scratch_shapes=[pltpu.VMEM((_compact_rows(tile_rows), d), BF16), pltpu.SemaphoreType.DMA(())],
        input_output_aliases={3: 0},
        name="moe_dispatch",
        compiler_params=_cparams(("arbitrary",)),
    )(plan.runs, plan.local_dest.reshape(n_tiles, 2, tile_rows), x, xs)


def _expert_kernel(te_ref, used_ref, xs_ref, wg_ref, wu_ref, wd_ref, o_ref, acc_ref):
    del te_ref
    t = pl.program_id(0)
    j = pl.program_id(1)
    last = pl.num_programs(1) - 1

    @pl.when(t < used_ref[0])
    def _():
        piece = _swiglu_piece(xs_ref[...], wg_ref[...], wu_ref[...], wd_ref[...])

        @pl.when(j == 0)
        def _():
            acc_ref[...] = piece

        @pl.when(jnp.logical_and(j > 0, j < last))
        def _():
            acc_ref[...] += piece

        @pl.when(j == last)
        def _():
            o_ref[...] = (acc_ref[...] + piece).astype(o_ref.dtype)

    @pl.when(jnp.logical_and(t >= used_ref[0], j == last))
    def _():
        o_ref[...] = jnp.zeros(o_ref.shape, o_ref.dtype)


def _experts(xs, plan, w_gu, w_down):
    r, d = xs.shape
    d_ff = w_down.shape[1]
    piece = d_ff // FFN_SPLIT
    layer = w_gu.layer
    assert FFN_SPLIT >= 2
    grid_spec = pltpu.PrefetchScalarGridSpec(
        num_scalar_prefetch=2,
        grid=(r // EXPERT_ROWS, FFN_SPLIT),
        in_specs=[pl.BlockSpec((EXPERT_ROWS, d), lambda t, j, te, used: (t, 0)),
                  pl.BlockSpec((None, None, d, piece), lambda t, j, te, used: (layer, te[t], 0, j)),
                  pl.BlockSpec((None, None, d, piece), lambda t, j, te, used: (layer, te[t], 0, FFN_SPLIT + j)),
                  pl.BlockSpec((None, None, piece, d), lambda t, j, te, used: (layer, te[t], j, 0))],
        out_specs=pl.BlockSpec((EXPERT_ROWS, d), lambda t, j, te, used: (t, 0)),
        scratch_shapes=[pltpu.VMEM((EXPERT_ROWS, d), F32)],
    )
    return pl.pallas_call(
        _expert_kernel,
        grid_spec=grid_spec,
        out_shape=jax.ShapeDtypeStruct((r, d), BF16),
        name="moe_experts",
        compiler_params=_cparams(("parallel", "arbitrary")),
    )(plan.tile_expert, plan.n_used, xs, w_gu.array, w_gu.array, w_down.array)


def _combine_kernel(runs_ref, next_runs_ref, x_ref, dest_ref, gate_ref, ys_ref, g_ref, b_ref, o_ref,
                    buf_ref, sem, *, alpha):
    i = pl.program_id(0)
    half = lax.rem(i, 2)

    def copy_into(which):
        def copy(local, src):
            return pltpu.make_async_copy(ys_ref.at[pl.ds(src, RUN_ROWS)],
                                         buf_ref.at[which, pl.ds(local, RUN_ROWS)], sem.at[which])
        return copy

    @pl.when(i == 0)
    def _():
        buf_ref[...] = jnp.zeros(buf_ref.shape, buf_ref.dtype)
        _start_runs(runs_ref, copy_into(0))

    @pl.when(i + 1 < pl.num_programs(0))
    def _():
        _start_runs(next_runs_ref, copy_into(1 - half))

    _wait_runs(runs_ref, copy_into(half))
    y = buf_ref[half]
    col = lax.broadcasted_iota(jnp.int32, (x_ref.shape[0], y.shape[0]), 1)
    gates = gate_ref[...]
    mixed = None
    for slot in range(2):
        pick = jnp.where(col == dest_ref[:, slot:slot + 1], 1.0, 0.0).astype(BF16)
        term = gates[:, slot:slot + 1] * jnp.dot(pick, y, preferred_element_type=F32)
        mixed = term if mixed is None else mixed + term
    o_ref[...] = _layer_norm(alpha * x_ref[...] + mixed, g_ref[...], b_ref[...])


def _combine(x, plan, gates_t, ys, ln_g, ln_b, *, tile_rows, alpha):
    n, d = x.shape
    n_tiles = n // tile_rows
    row_spec = pl.BlockSpec((tile_rows, d), lambda i: (i, 0))
    pair_spec = pl.BlockSpec((tile_rows, 2), lambda i: (i, 0))
    dest_cols = plan.local_dest.reshape(n_tiles, 2, tile_rows).transpose(0, 2, 1).reshape(n, 2)
    runs_shape = (None, 1, 3 * N_EXPERTS)
    return pl.pallas_call(
        functools.partial(_combine_kernel, alpha=alpha),
        grid=(n_tiles,),
        in_specs=[pl.BlockSpec(runs_shape, lambda i: (i, 0, 0), memory_space=pltpu.SMEM),
                  pl.BlockSpec(runs_shape, lambda i: (jnp.minimum(i + 1, n_tiles - 1), 0, 0),
                               memory_space=pltpu.SMEM),
                  row_spec, pair_spec, pair_spec,
                  pl.BlockSpec(memory_space=pl.ANY),
                  _resident(ln_g), _resident(ln_b)],
        out_specs=row_spec,
        out_shape=jax.ShapeDtypeStruct((n, d), F32),
        scratch_shapes=[pltpu.VMEM((2, _compact_rows(tile_rows), d), ys.dtype),
                        pltpu.SemaphoreType.DMA((2,))],
        name="moe_combine",
        compiler_params=_cparams(("arbitrary",)),
    )(plan.runs, plan.runs, x, dest_cols, gates_t, ys, ln_g, ln_b)


def _moe_layer(streams, w_router_t, w_gu, w_down, ln_g, ln_b, *, alpha):
    d = streams[0].shape[1]
    tile_rows = min([MOVE_ROWS] + [x.shape[0] for x in streams])
    assert all(x.shape[0] % tile_rows == 0 for x in streams)
    routed = [_router(x, w_router_t) for x in streams]
    plan = _route_plan(jnp.concatenate([idx for idx, _ in routed], axis=1), tile_rows)
    tile_lo, parts = 0, []
    for x in streams:
        tile_hi = tile_lo + x.shape[0] // tile_rows
        parts.append(plan._replace(local_dest=plan.local_dest[tile_lo:tile_hi], runs=plan.runs[tile_lo:tile_hi]))
        tile_lo = tile_hi
    xs = jnp.zeros((plan.tile_expert.shape[0] * EXPERT_ROWS, d), BF16)
    for x, part in zip(streams, parts):
        xs = _dispatch(x, part, tile_rows, xs)
    ys = _experts(xs, plan, w_gu, w_down)
    return [_combine(x, part, gates.T, ys, ln_g, ln_b, tile_rows=tile_rows, alpha=alpha)
            for x, part, (_, gates) in zip(streams, parts, routed)]


def _kv_kernel(x_ref, w_ref, wvt_ref, k_ref, v_ref, kb_ref, vt_ref):
    qk_w = kb_ref.shape[-1]
    xb = x_ref[...].astype(BF16)
    kv = jnp.dot(xb, w_ref[...], preferred_element_type=F32)
    k = kv[:, :qk_w]
    k_ref[...] = k
    v_ref[...] = kv[:, qk_w:]
    kb_ref[...] = k.astype(BF16)
    vt_ref[...] = _dot_nt(wvt_ref[...], xb).astype(BF16)


def _kv_proj(x, w_kv, w_vt):
    bsz, t, d = x.shape
    qk_w = w_kv.shape[1] // 2
    rows = min(ATT_K, t)
    assert t % rows == 0
    row_spec = pl.BlockSpec((None, rows, qk_w), lambda b, i: (b, i, 0))
    return pl.pallas_call(
        _kv_kernel,
        grid=(bsz, t // rows),
        in_specs=[pl.BlockSpec((None, rows, d), lambda b, i: (b, i, 0)),
                  _resident(w_kv), _resident(w_vt)],
        out_specs=[row_spec, row_spec, row_spec,
                   pl.BlockSpec((None, None, qk_w, rows), lambda b, i: (b, i, 0, 0))],
        out_shape=[jax.ShapeDtypeStruct((bsz, t, qk_w), F32),
                   jax.ShapeDtypeStruct((bsz, t, qk_w), F32),
                   jax.ShapeDtypeStruct((bsz, t, qk_w), BF16),
                   jax.ShapeDtypeStruct((bsz, t // rows, qk_w, rows), BF16)],
        name="kv_proj",
        compiler_params=_cparams(("parallel", "parallel")),
    )(x, w_kv, w_vt)


def _t5_bucket(rel):
    nb = NUM_BUCKETS // 2
    max_exact = nb // 2
    ret = (rel > 0).astype(jnp.int32) * nb
    n = jnp.abs(rel)
    nf = jnp.maximum(n, 1).astype(jnp.float32)
    large = max_exact + (jnp.log(nf / max_exact) / math.log(MAX_DISTANCE / max_exact)
                         * (nb - max_exact)).astype(jnp.int32)
    large = jnp.minimum(large, nb - 1)
    return ret + jnp.where(n < max_exact, n, large)


def _bucket_table(q_pos, k_pos, visible=None):
    bucket = _t5_bucket(k_pos[:, None] - q_pos[None, :])
    return bucket if visible is None else jnp.where(visible, bucket, -1)


def _bias_kernel(far_ref, rb_ref, bucket_ref, o_ref):
    h = pl.program_id(0)
    bucket = bucket_ref[...]
    far = rb_ref[far_ref[0], h]
    out = jnp.zeros(bucket.shape, F32)
    for b in range(NUM_BUCKETS):
        out = jnp.where(bucket == b, rb_ref[b, h] - far, out)
    o_ref[...] = jnp.where(bucket < 0, NEG_INF, out * LOG2_E)


def _bias_tables(rel_bias, buckets):
    n_heads2 = rel_bias.shape[1]
    far_bucket = _t5_bucket(jnp.full((1,), -MAX_DISTANCE, jnp.int32))
    return pl.pallas_call(
        _bias_kernel,
        grid=(n_heads2,),
        in_specs=[pl.BlockSpec(memory_space=pltpu.SMEM), pl.BlockSpec(memory_space=pltpu.SMEM),
                  _resident(buckets)],
        out_specs=pl.BlockSpec((None,) + buckets.shape, lambda h: (h, 0, 0)),
        out_shape=jax.ShapeDtypeStruct((n_heads2,) + buckets.shape, F32),
        name="bias_tables",
        compiler_params=_cparams(("parallel",)),
    )(far_bucket, rel_bias.astype(F32), buckets)


def _att_init(x_ref, wqt_ref, qa_ref, qb_ref, m_ref, l_ref, acc_ref):
    qt = _dot_nt(wqt_ref[...], x_ref[...].astype(BF16)) * (HEAD_DIM ** -0.5 * LOG2_E)
    row = lax.broadcasted_iota(jnp.int32, (LANES, qt.shape[1]), 0)
    for h in range(qa_ref.shape[0]):
        slab = qt[h * LANES:(h + 1) * LANES, :]
        qa_ref[h] = jnp.where(row < HEAD_DIM, slab, 0.0).astype(BF16)
        qb_ref[h] = jnp.where(row >= HEAD_DIM, slab, 0.0).astype(BF16)
    m_ref[...] = jnp.full(m_ref.shape, NEG_INF, F32)
    l_ref[...] = jnp.zeros(l_ref.shape, F32)
    acc_ref[...] = jnp.zeros(acc_ref.shape, F32)


def _att_block(qa_ref, qb_ref, k, vt_slab, bias, m_ref, l_ref, acc_ref):
    n_heads = 2 * qa_ref.shape[0]

    def scores(hh):
        h = hh // 2
        q_ref = qa_ref if hh % 2 == 0 else qb_ref
        s = jnp.dot(k[:, h * LANES:(h + 1) * LANES], q_ref[h], preferred_element_type=F32)
        return s if bias is None else s + bias(hh)

    s_of, p_of, scale_of = {}, {}, {}

    def softmax(hh):
        s = s_of.pop(hh)
        m_old = m_ref[hh]
        m_new = jnp.maximum(m_old, jnp.max(s, axis=0, keepdims=True))
        scale = jnp.exp2(m_old - m_new)
        p = jnp.exp2(s - m_new)
        l_ref[hh] = scale * l_ref[hh] + jnp.sum(p, axis=0, keepdims=True)
        m_ref[hh] = m_new
        p_of[hh] = p.astype(BF16)
        scale_of[hh] = scale

    def accumulate(hh):
        acc_ref[hh] = scale_of.pop(hh) * acc_ref[hh] + jnp.dot(vt_slab(hh // 2), p_of.pop(hh),
                                                               preferred_element_type=F32)

    n_groups = n_heads // HEAD_GROUP
    group = lambda t: range(t * HEAD_GROUP, (t + 1) * HEAD_GROUP)
    for t in range(-2, n_groups):
        if t + 2 < n_groups:
            for hh in group(t + 2):
                s_of[hh] = scores(hh)
        if 0 <= t + 1 < n_groups:
            for hh in group(t + 1):
                softmax(hh)
        if t >= 0:
            for hh in group(t):
                accumulate(hh)


def _att_finish(x_ref, lam_ref, sg_ref, wout_ref, g_ref, b_ref, o_ref, l_ref, acc_ref, heads_ref,
                *, lam_init, alpha):
    lp = lam_ref[...]
    lam = (jnp.exp(jnp.sum(lp[0:1] * lp[1:2], axis=-1, keepdims=True))
           - jnp.exp(jnp.sum(lp[2:3] * lp[3:4], axis=-1, keepdims=True)) + lam_init)
    for h in range(acc_ref.shape[0] // 2):
        a = acc_ref[2 * h] * (1.0 / l_ref[2 * h]) - (lam / l_ref[2 * h + 1]) * acc_ref[2 * h + 1]
        r = a * lax.rsqrt(jnp.mean(a * a, axis=0, keepdims=True) + LN_EPS) * sg_ref[...]
        heads_ref[h * LANES:(h + 1) * LANES, :] = r * (1.0 - lam_init)
    y = lax.dot_general(heads_ref[...].astype(BF16), wout_ref[...], (((0,), (0,)), ((), ())),
                        preferred_element_type=F32)
    o_ref[...] = _layer_norm(alpha * x_ref[...] + y, g_ref[...], b_ref[...])


def _att_scratch(q_rows, width):
    n_slabs = width // LANES
    return [pltpu.VMEM((n_slabs, LANES, q_rows), BF16), pltpu.VMEM((n_slabs, LANES, q_rows), BF16),
            pltpu.VMEM((2 * n_slabs, 1, q_rows), F32), pltpu.VMEM((2 * n_slabs, 1, q_rows), F32),
            pltpu.VMEM((2 * n_slabs, LANES, q_rows), F32), pltpu.VMEM((width, q_rows), F32)]


STEP_FAR_WIDE = 0
STEP_FAR_ONE = 1
STEP_NEAR = 2
FAR_TILES = 4


def _prompt_att_kernel(qi_ref, kind_ref, first_ref, wide_ref, ka_ref_idx, kb_ref_idx,
                       x_ref, kw_ref, vtw_ref, ka_ref, vta_ref, kb_ref, vtb_ref, wqt_ref, bias_ref, lam_ref,
                       sg_ref, wout_ref, g_ref, b_ref, o_ref, qa_ref, qb_ref, m_ref, l_ref, acc_ref,
                       heads_ref, *, lam_init, alpha):
    del wide_ref, ka_ref_idx, kb_ref_idx
    p = pl.program_id(1)
    kind = kind_ref[p]
    tk = ka_ref.shape[0]
    state = (m_ref, l_ref, acc_ref)

    def slab(ref, h):
        return ref[h * LANES:(h + 1) * LANES, :]

    @pl.when(first_ref[p] == 1)
    def _():
        _att_init(x_ref, wqt_ref, qa_ref, qb_ref, *state)

    @pl.when(kind == STEP_FAR_WIDE)
    def _():
        def vt_slab(h):
            return jnp.concatenate([vtw_ref[c, h * LANES:(h + 1) * LANES, :]
                                    for c in range(vtw_ref.shape[0])], axis=1)
        _att_block(qa_ref, qb_ref, kw_ref[...], vt_slab, None, *state)

    @pl.when(kind == STEP_FAR_ONE)
    def _():
        _att_block(qa_ref, qb_ref, ka_ref[...], lambda h: slab(vta_ref, h), None, *state)

    @pl.when(kind == STEP_NEAR)
    def _():
        hide = jnp.where(qi_ref[p] == 0, NEG_INF, 0.0)

        def bias(hh):
            return jnp.concatenate([bias_ref[hh, 0:tk, :] + hide, bias_ref[hh, tk:2 * tk, :]], axis=0)

        def vt_slab(h):
            return jnp.concatenate([slab(vta_ref, h), slab(vtb_ref, h)], axis=1)

        k = jnp.concatenate([ka_ref[...], kb_ref[...]], axis=0)
        _att_block(qa_ref, qb_ref, k, vt_slab, bias, *state)
        _att_finish(x_ref, lam_ref, sg_ref, wout_ref, g_ref, b_ref, o_ref, l_ref, acc_ref, heads_ref,
                    lam_init=lam_init, alpha=alpha)


def _prompt_bias(rel_bias, tq):
    pos = jnp.arange(tq, dtype=jnp.int32)
    visible = pos[:, None] < (pos[None, :] // CHUNK + 1) * CHUNK
    buckets = jnp.concatenate([_bucket_table(pos + tq, pos), _bucket_table(pos, pos, visible)], axis=0)
    return _bias_tables(rel_bias, buckets)


def _prompt_steps(n_q):
    steps = []
    wide = ka = kb = 0
    for i in range(n_q):
        n_far = max(i - 1, 0)
        first = 1
        for c in range(n_far // FAR_TILES):
            wide = c
            steps.append((i, STEP_FAR_WIDE, first, wide, ka, kb))
            first = 0
        for j in range(n_far - n_far % FAR_TILES, n_far):
            ka = j
            steps.append((i, STEP_FAR_ONE, first, wide, ka, kb))
            first = 0
        ka, kb = max(i - 1, 0), i
        steps.append((i, STEP_NEAR, first, wide, ka, kb))
    return [jnp.array(col, jnp.int32) for col in zip(*steps)]


def _prompt_attention(x, kb, vt, bias, w_qt, lam_p, subln_g, w_out, ln_g, ln_b, *, lam_init, alpha):
    bsz, s_len, d = x.shape
    width = kb.shape[-1]
    tq = vt.shape[-1]
    n_q = s_len // tq
    assert s_len % tq == 0 and tq % CHUNK == 0 and tq >= MAX_DISTANCE
    assert n_q % FAR_TILES == 0 or n_q <= FAR_TILES
    steps = _prompt_steps(n_q)
    wide_tiles = min(FAR_TILES, n_q)
    x_spec = pl.BlockSpec((None, tq, d), lambda b, p, qi, kind, first, wide, ka, kb: (b, qi[p], 0))
    grid_spec = pltpu.PrefetchScalarGridSpec(
        num_scalar_prefetch=len(steps),
        grid=(bsz, steps[0].shape[0]),
        in_specs=[x_spec,
                  pl.BlockSpec((None, wide_tiles * tq, width),
                               lambda b, p, qi, kind, first, wide, ka, kb: (b, wide[p], 0)),
                  pl.BlockSpec((None, wide_tiles, width, tq),
                               lambda b, p, qi, kind, first, wide, ka, kb: (b, wide[p], 0, 0)),
                  pl.BlockSpec((None, tq, width), lambda b, p, qi, kind, first, wide, ka, kb: (b, ka[p], 0)),
                  pl.BlockSpec((None, None, width, tq),
                               lambda b, p, qi, kind, first, wide, ka, kb: (b, ka[p], 0, 0)),
                  pl.BlockSpec((None, tq, width), lambda b, p, qi, kind, first, wide, ka, kb: (b, kb[p], 0)),
                  pl.BlockSpec((None, None, width, tq),
                               lambda b, p, qi, kind, first, wide, ka, kb: (b, kb[p], 0, 0)),
                  _resident(w_qt), _resident(bias), _resident(lam_p), _resident(subln_g), _resident(w_out),
                  _resident(ln_g), _resident(ln_b)],
        out_specs=x_spec,
        scratch_shapes=_att_scratch(tq, width),
    )
    return pl.pallas_call(
        functools.partial(_prompt_att_kernel, lam_init=lam_init, alpha=alpha),
        grid_spec=grid_spec,
        out_shape=jax.ShapeDtypeStruct((bsz, s_len, d), F32),
        name="prompt_attention",
        compiler_params=_cparams(("parallel", "arbitrary")),
    )(*steps, x, kb, vt, kb, vt, kb, vt, _operand(w_qt), bias, lam_p, subln_g, _operand(w_out), ln_g, ln_b)


def _sample_att_kernel(x_ref, ck_ref, cv_ref, kn_ref, vtn_ref, wqt_ref, bias_ref, lam_ref, sg_ref,
                       wout_ref, g_ref, b_ref, o_ref, qa_ref, qb_ref, m_ref, l_ref, acc_ref, heads_ref,
                       *, lam_init, alpha):
    j = pl.program_id(1)
    n_cache = pl.num_programs(1) - 1
    tk = ck_ref.shape[0]
    t_new = kn_ref.shape[0]
    state = (m_ref, l_ref, acc_ref)

    def near_bias(hh):
        return bias_ref[hh, 0:tk, :]

    def new_bias(hh):
        return bias_ref[hh, tk:tk + t_new, :]

    def cached_vt_slab(h):
        return jnp.transpose(cv_ref[:, h * LANES:(h + 1) * LANES]).astype(BF16)

    def new_vt_slab(h):
        return vtn_ref[h * LANES:(h + 1) * LANES, :]

    @pl.when(j == 0)
    def _():
        _att_init(x_ref, wqt_ref, qa_ref, qb_ref, *state)

    @pl.when(j < n_cache - 1)
    def _():
        _att_block(qa_ref, qb_ref, ck_ref[...].astype(BF16), cached_vt_slab, None, *state)

    @pl.when(j == n_cache - 1)
    def _():
        _att_block(qa_ref, qb_ref, ck_ref[...].astype(BF16), cached_vt_slab, near_bias, *state)

    @pl.when(j == n_cache)
    def _():
        _att_block(qa_ref, qb_ref, kn_ref[...], new_vt_slab, new_bias, *state)
        _att_finish(x_ref, lam_ref, sg_ref, wout_ref, g_ref, b_ref, o_ref, l_ref, acc_ref, heads_ref,
                    lam_init=lam_init, alpha=alpha)


def _sample_bias(rel_bias, past, t):
    tk = min(CACHE_K, past)
    q_pos = past + jnp.arange(t, dtype=jnp.int32)
    buckets = jnp.concatenate([_bucket_table(q_pos, past - tk + jnp.arange(tk, dtype=jnp.int32)),
                               _bucket_table(q_pos, q_pos)], axis=0)
    return _bias_tables(rel_bias, buckets)


def _sample_attention(x, cache_k, cache_v, kb, vt, bias, w_qt, lam_p, subln_g, w_out, ln_g, ln_b,
                      *, lam_init, alpha):
    bsz, t, d = x.shape
    past = cache_k.shape[1]
    width = kb.shape[-1]
    tk = min(CACHE_K, past)
    assert past % tk == 0 and tk >= MAX_DISTANCE and vt.shape[1] == 1
    n_cache = past // tk
    cache_spec = pl.BlockSpec((None, tk, width), lambda b, j: (b, jnp.minimum(j, n_cache - 1), 0))
    x_spec = pl.BlockSpec((None, t, d), lambda b, j: (b, 0, 0))
    return pl.pallas_call(
        functools.partial(_sample_att_kernel, lam_init=lam_init, alpha=alpha),
        grid=(bsz, n_cache + 1),
        in_specs=[x_spec, cache_spec, cache_spec,
                  pl.BlockSpec((None, t, width), lambda b, j: (b, 0, 0)),
                  pl.BlockSpec((None, None, width, t), lambda b, j: (b, 0, 0, 0)),
                  _resident(w_qt), _resident(bias), _resident(lam_p), _resident(subln_g), _resident(w_out),
                  _resident(ln_g), _resident(ln_b)],
        out_specs=x_spec,
        out_shape=jax.ShapeDtypeStruct((bsz, t, d), F32),
        scratch_shapes=_att_scratch(t, width),
        name="sample_attention",
        compiler_params=_cparams(("parallel", "arbitrary")),
    )(x, cache_k, cache_v, kb, vt, _operand(w_qt), bias, lam_p, subln_g, _operand(w_out), ln_g, ln_b)


class _Stream:
    def __init__(self, x, cache_k, cache_v, emit_v):
        self.bsz, self.t, self.d = x.shape
        self.h = x.reshape(self.bsz * self.t, self.d)
        self.cache_k, self.cache_v, self.emit_v = cache_k, cache_v, emit_v
        self.sgu_rows = []
        self.k_new = self.v_new = self.kb = self.vt = self.bias = None


def _mixer(s, i, w, *, depth, alpha):
    n_a = depth // 2
    if i < n_a:
        chunk = min(s.t, SGU_CHUNK)
        s.h, v_rows = _sgu_layer(
            s.h, _Stacked(w["a_w_in"], i), w["a_ln_g"][i], w["a_ln_b"][i], w["a_w_s"][i][:, :chunk, :chunk],
            w["a_b_s"][i][:, :chunk].T, _Stacked(w["a_w_out"], i), w["ln_mix_g"][i], w["ln_mix_b"][i],
            chunk=chunk, alpha=alpha, emit_v=s.emit_v)
        s.sgu_rows.append(v_rows)
        return
    h3 = s.h.reshape(s.bsz, s.t, s.d)
    if s.k_new is None:
        s.k_new, s.v_new, s.kb, s.vt = _kv_proj(h3, w["w_kv"], w["w_vt"])
        if s.cache_k is None:
            s.bias = _prompt_bias(w["rel_bias"], s.vt.shape[-1])
        else:
            s.bias = _sample_bias(w["rel_bias"], s.cache_k.shape[1], s.t)
    j = i - n_a
    lam_init = 0.8 - 0.6 * math.exp(-0.3 * i)
    args = (s.kb, s.vt, s.bias, _Stacked(w["b_w_qt"], j), w["b_lam"][j], w["b_subln_g"][j],
            _Stacked(w["b_w_out"], j), w["ln_mix_g"][i], w["ln_mix_b"][i])
    if s.cache_k is None:
        out = _prompt_attention(h3, *args, lam_init=lam_init, alpha=alpha)
    else:
        past = s.cache_k.shape[1]
        out = _sample_attention(h3, s.cache_k.reshape(s.bsz, past, -1), s.cache_v.reshape(s.bsz, past, -1),
                                *args, lam_init=lam_init, alpha=alpha)
    s.h = out.reshape(s.bsz * s.t, s.d)


def _run_layers(streams, w, *, depth):
    alpha = (2 * depth) ** 0.25
    for i in range(depth):
        for s in streams:
            _mixer(s, i, w, depth=depth, alpha=alpha)
        if i % 2 == 0:
            for s in streams:
                s.h = _ffn_layer(s.h, _Stacked(w["ffn_w_gu"], i // 2), _Stacked(w["ffn_w_down"], i // 2),
                                 w["ln_ffn_g"][i], w["ln_ffn_b"][i], alpha=alpha)
        else:
            outs = _moe_layer([s.h for s in streams], w["moe_w_router_t"][i // 2],
                              _Stacked(w["moe_w_gu"], i // 2), _Stacked(w["moe_w_down"], i // 2),
                              w["ln_ffn_g"][i], w["ln_ffn_b"][i], alpha=alpha)
            for s, out in zip(streams, outs):
                s.h = out


def kernel(x_prompt, x_sample, cache_k, cache_v, a_w_in, a_ln_g, a_ln_b, a_w_s, a_b_s, a_w_out, w_kv, b_w_q,
           b_lam, b_subln_g, b_w_out, rel_bias, ln_mix_g, ln_mix_b, ln_ffn_g, ln_ffn_b, ffn_w_gu, ffn_w_down,
           moe_w_router, moe_w_gu, moe_w_down):
    depth = ln_mix_g.shape[0]
    row = lambda a: a[:, None, :]
    w = dict(
        a_w_in=a_w_in.astype(BF16), a_ln_g=row(a_ln_g), a_ln_b=row(a_ln_b), a_w_s=a_w_s, a_b_s=a_b_s,
        a_w_out=a_w_out.astype(BF16), w_kv=w_kv.astype(BF16),
        w_vt=jnp.transpose(w_kv[:, w_kv.shape[1] // 2:]).astype(BF16),
        b_w_qt=jnp.swapaxes(b_w_q, 1, 2).astype(BF16), b_lam=b_lam,
        b_subln_g=b_subln_g[:, :, None], b_w_out=b_w_out.astype(BF16), rel_bias=rel_bias,
        ln_mix_g=row(ln_mix_g), ln_mix_b=row(ln_mix_b), ln_ffn_g=row(ln_ffn_g), ln_ffn_b=row(ln_ffn_b),
        ffn_w_gu=ffn_w_gu.astype(BF16), ffn_w_down=ffn_w_down.astype(BF16),
        moe_w_router_t=jnp.swapaxes(moe_w_router, 1, 2), moe_w_gu=moe_w_gu.astype(BF16),
        moe_w_down=moe_w_down.astype(BF16))
    prompt = _Stream(x_prompt, None, None, emit_v=False)
    sample = _Stream(x_sample, cache_k, cache_v, emit_v=True)
    _run_layers([prompt, sample], w, depth=depth)
    d_sgu = a_w_out.shape[1]
    k_heads = (rel_bias.shape[1], HEAD_DIM)
    v_heads = (rel_bias.shape[1] // 2, 2 * HEAD_DIM)
    return (prompt.h.reshape(x_prompt.shape), sample.h.reshape(x_sample.shape),
            prompt.k_new.reshape(x_prompt.shape[:2] + k_heads), prompt.v_new.reshape(x_prompt.shape[:2] + v_heads),
            sample.k_new.reshape(x_sample.shape[:2] + k_heads), sample.v_new.reshape(x_sample.shape[:2] + v_heads),
            jnp.stack([r.reshape(x_sample.shape[:2] + (d_sgu,)) for r in sample.sgu_rows], axis=0))
```

```python
import functools
import math
from typing import NamedTuple

import jax
import jax.numpy as jnp
from jax import lax
from jax.experimental import pallas as pl
from jax.experimental.pallas import tpu as pltpu

F32 = jnp.float32
BF16 = jnp.bfloat16

CHUNK = 64
SGU_CHUNK = 128
SGU_GROUPS = 4
HEAD_DIM = 64
NUM_BUCKETS = 32
MAX_DISTANCE = 128
N_EXPERTS = 8
LN_EPS = 1e-5
NEG_INF = -1e30
SQRT_HALF = math.sqrt(0.5)
LOG2_E = math.log2(math.e)

LANES = 128
VMEM_LIMIT_BYTES = 56 * 1024 * 1024

SGU_ROWS = 256
FFN_ROWS = 512
FFN_SPLIT = 2
ROUTE_ROWS = 1024
MOVE_ROWS = 512
RUN_ROWS = 16
EXPERT_ROWS = 512
ATT_K = 256
CACHE_K = 512
HEAD_GROUP = 2


def _cparams(semantics):
    return pltpu.CompilerParams(dimension_semantics=semantics, vmem_limit_bytes=VMEM_LIMIT_BYTES)


class _Stacked(NamedTuple):
    array: jax.Array
    layer: int

    @property
    def shape(self):
        return self.array.shape[1:]


def _operand(a):
    return a.array if isinstance(a, _Stacked) else a


def _resident(a):
    nd = len(a.shape)
    if isinstance(a, _Stacked):
        return pl.BlockSpec((None,) + a.shape, lambda *_: (a.layer,) + (0,) * nd,
                            pipeline_mode=pl.Buffered(1))
    return pl.BlockSpec(a.shape, lambda *_: (0,) * nd, pipeline_mode=pl.Buffered(1))


def _layer_norm(z, g, b):
    mu = jnp.mean(z, axis=-1, keepdims=True)
    zc = z - mu
    var = jnp.mean(zc * zc, axis=-1, keepdims=True)
    return zc * lax.rsqrt(var + LN_EPS) * g + b


def _dot_nt(a, b):
    return lax.dot_general(a, b, (((1,), (1,)), ((), ())), preferred_element_type=F32)


def _gelu_exact(h):
    return 0.5 * h * (1.0 + lax.erf(h * SQRT_HALF))


def _sgu_kernel(x_ref, win_ref, lg_ref, lb_ref, ws_ref, bs_ref, wout_ref, mg_ref, mb_ref,
                o_ref, *rest, chunk, alpha, emit_v):
    if emit_v:
        v_ref, gate_ref = rest
    else:
        (gate_ref,) = rest
    rows = x_ref.shape[0]
    d_sgu = wout_ref.shape[0]
    gw = d_sgu // SGU_GROUPS
    x = x_ref[...]
    xb = x.astype(BF16)
    v = _gelu_exact(jnp.dot(xb, win_ref[:, d_sgu:], preferred_element_type=F32))
    vn = _layer_norm(v, lg_ref[...], lb_ref[...])
    if emit_v:
        v_ref[...] = vn
    vb = vn.astype(BF16)
    u = _gelu_exact(jnp.dot(xb, win_ref[:, :d_sgu], preferred_element_type=F32))
    tril = (lax.broadcasted_iota(jnp.int32, (chunk, chunk), 0)
            >= lax.broadcasted_iota(jnp.int32, (chunk, chunk), 1))
    for g in range(SGU_GROUPS):
        wg = jnp.where(tril, ws_ref[g], 0.0).astype(BF16)
        bcol = bs_ref[:, g:g + 1]
        for c in range(rows // chunk):
            r0, r1, c0, c1 = c * chunk, (c + 1) * chunk, g * gw, (g + 1) * gw
            mixed = jnp.dot(wg, vb[r0:r1, c0:c1], preferred_element_type=F32) + bcol
            gate_ref[r0:r1, c0:c1] = (u[r0:r1, c0:c1] * mixed).astype(BF16)
    y = jnp.dot(gate_ref[...], wout_ref[...], preferred_element_type=F32)
    o_ref[...] = _layer_norm(alpha * x + y, mg_ref[...], mb_ref[...])


def _sgu_layer(x, w_in, ln_g, ln_b, w_s, b_s, w_out, mix_g, mix_b, *, chunk, alpha, emit_v):
    n, d = x.shape
    d_sgu = w_out.shape[0]
    rows = min(SGU_ROWS, n)
    assert n % rows == 0 and rows % chunk == 0
    row_spec = pl.BlockSpec((rows, d), lambda i: (i, 0))
    out_shape = [jax.ShapeDtypeStruct((n, d), F32)]
    out_specs = [row_spec]
    if emit_v:
        out_shape.append(jax.ShapeDtypeStruct((n, d_sgu), F32))
        out_specs.append(pl.BlockSpec((rows, d_sgu), lambda i: (i, 0)))
    res = pl.pallas_call(
        functools.partial(_sgu_kernel, chunk=chunk, alpha=alpha, emit_v=emit_v),
        grid=(n // rows,),
        in_specs=[row_spec] + [_resident(a) for a in (w_in, ln_g, ln_b, w_s, b_s, w_out, mix_g, mix_b)],
        out_specs=out_specs,
        out_shape=out_shape,
        scratch_shapes=[pltpu.VMEM((rows, d_sgu), BF16)],
        name="sgu_layer",
        compiler_params=_cparams(("parallel",)),
    )(x, _operand(w_in), ln_g, ln_b, w_s, b_s, _operand(w_out), mix_g, mix_b)
    return res if emit_v else (res[0], None)


def _swiglu_piece(xb, wg, wu, wd):
    g = jnp.dot(xb, wg, preferred_element_type=F32)
    u = jnp.dot(xb, wu, preferred_element_type=F32)
    a = (g * jax.nn.sigmoid(g) * u).astype(BF16)
    return jnp.dot(a, wd, preferred_element_type=F32)


def _ffn_kernel(x_ref, wgu_ref, wd_ref, g_ref, b_ref, o_ref, *, alpha):
    d_ff = wd_ref.shape[0]
    piece = d_ff // FFN_SPLIT
    x = x_ref[...]
    xb = x.astype(BF16)
    acc = None
    for j in range(FFN_SPLIT):
        t = _swiglu_piece(xb, wgu_ref[:, j * piece:(j + 1) * piece],
                          wgu_ref[:, d_ff + j * piece:d_ff + (j + 1) * piece],
                          wd_ref[j * piece:(j + 1) * piece, :])
        acc = t if acc is None else acc + t
    o_ref[...] = _layer_norm(alpha * x + acc, g_ref[...], b_ref[...])


def _ffn_layer(x, w_gu, w_down, ln_g, ln_b, *, alpha):
    n, d = x.shape
    rows = min(FFN_ROWS, n)
    assert n % rows == 0 and w_down.shape[0] % (FFN_SPLIT * LANES) == 0
    row_spec = pl.BlockSpec((rows, d), lambda i: (i, 0))
    return pl.pallas_call(
        functools.partial(_ffn_kernel, alpha=alpha),
        grid=(n // rows,),
        in_specs=[row_spec] + [_resident(a) for a in (w_gu, w_down, ln_g, ln_b)],
        out_specs=row_spec,
        out_shape=jax.ShapeDtypeStruct((n, d), F32),
        name="dense_ffn",
        compiler_params=_cparams(("parallel",)),
    )(x, _operand(w_gu), _operand(w_down), ln_g, ln_b)


def _router_kernel(x_ref, wr_ref, idx_ref, gate_ref):
    logits = _dot_nt(wr_ref[...].astype(BF16), x_ref[...].astype(BF16))
    ids = lax.broadcasted_iota(jnp.int32, logits.shape, 0)
    m1 = jnp.max(logits, axis=0, keepdims=True)
    i1 = jnp.min(jnp.where(logits == m1, ids, N_EXPERTS), axis=0, keepdims=True)
    rest = jnp.where(ids == i1, -jnp.inf, logits)
    m2 = jnp.max(rest, axis=0, keepdims=True)
    i2 = jnp.min(jnp.where(rest == m2, ids, N_EXPERTS), axis=0, keepdims=True)
    e2 = jnp.exp(m2 - m1)
    den = 1.0 + e2
    idx_ref[...] = jnp.concatenate([i1, i2], axis=0)
    gate_ref[...] = jnp.concatenate([1.0 / den, e2 / den], axis=0)


def _router(x, w_router_t):
    n, d = x.shape
    rows = min(ROUTE_ROWS, n)
    assert n % rows == 0
    return pl.pallas_call(
        _router_kernel,
        grid=(n // rows,),
        in_specs=[pl.BlockSpec((rows, d), lambda i: (i, 0)), _resident(w_router_t)],
        out_specs=[pl.BlockSpec((2, rows), lambda i: (0, i)), pl.BlockSpec((2, rows), lambda i: (0, i))],
        out_shape=[jax.ShapeDtypeStruct((2, n), jnp.int32), jax.ShapeDtypeStruct((2, n), F32)],
        name="moe_router",
        compiler_params=_cparams(("parallel",)),
    )(x, w_router_t)


class _RoutePlan(NamedTuple):
    local_dest: jax.Array
    runs: jax.Array
    tile_expert: jax.Array
    n_used: jax.Array


def _compact_rows(tile_rows):
    bound = 2 * tile_rows + N_EXPERTS * (RUN_ROWS - 1)
    return -(-bound // LANES) * LANES


def _sorted_rows(n_tokens, tile_rows):
    bound = 2 * n_tokens + (n_tokens // tile_rows) * N_EXPERTS * (RUN_ROWS - 1)
    return -(-bound // EXPERT_ROWS) * EXPERT_ROWS + N_EXPERTS * EXPERT_ROWS


def _route_plan(idx, tile_rows):
    n = idx.shape[1]
    n_tiles = n // tile_rows
    experts = idx.reshape(2, n_tiles, tile_rows).transpose(1, 0, 2).reshape(n_tiles, 2 * tile_rows)
    onehot = (experts[:, :, None] == jnp.arange(N_EXPERTS, dtype=jnp.int32)).astype(jnp.int32)
    order = jnp.arange(2 * tile_rows, dtype=jnp.int32)
    tri = (order[:, None] >= order[None, :]).astype(BF16)
    csum = jnp.einsum("uv,ive->iue", tri, onehot.astype(BF16),
                      preferred_element_type=F32).astype(jnp.int32)
    count = csum[:, -1, :]
    run = -(-count // RUN_ROWS) * RUN_ROWS
    local_off = jnp.cumsum(run, axis=1) - run
    before = jnp.cumsum(run, axis=0) - run
    group = -(-jnp.sum(run, axis=0) // EXPERT_ROWS) * EXPERT_ROWS
    ends = jnp.cumsum(group)
    sorted_off = (ends - group)[None, :] + before
    local_dest = jnp.sum(onehot * (local_off[:, None, :] + csum - 1), axis=2)
    runs = jnp.concatenate([local_off, sorted_off, run // RUN_ROWS], axis=1)[:, None, :]
    tile_start = jnp.arange(_sorted_rows(n, tile_rows) // EXPERT_ROWS, dtype=jnp.int32) * EXPERT_ROWS
    tile_expert = jnp.sum((tile_start[:, None] >= ends[None, :]).astype(jnp.int32), axis=1)
    return _RoutePlan(local_dest.astype(jnp.int32), runs.astype(jnp.int32),
                      jnp.minimum(tile_expert, N_EXPERTS - 1).astype(jnp.int32),
                      (ends[-1:] // EXPERT_ROWS).astype(jnp.int32))


def _start_runs(runs_ref, copy):
    for e in range(N_EXPERTS):
        local, dest, chunks = runs_ref[0, e], runs_ref[0, N_EXPERTS + e], runs_ref[0, 2 * N_EXPERTS + e]

        def start(j, carry, local=local, dest=dest):
            copy(pl.multiple_of(local + j * RUN_ROWS, RUN_ROWS),
                 pl.multiple_of(dest + j * RUN_ROWS, RUN_ROWS)).start()
            return carry

        lax.fori_loop(0, chunks, start, 0)


def _wait_runs(runs_ref, copy):
    total = 0
    for e in range(N_EXPERTS):
        total = total + runs_ref[0, 2 * N_EXPERTS + e]

    def wait(j, carry):
        copy(0, 0).wait()
        return carry

    lax.fori_loop(0, total, wait, 0)


def _dispatch_kernel(runs_ref, dest_ref, x_ref, xs_in_ref, xs_ref, buf_ref, sem):
    del xs_in_ref
    row = lax.broadcasted_iota(jnp.int32, (buf_ref.shape[0], x_ref.shape[0]), 0)
    pick = jnp.where(row == dest_ref[0:1, :], 1.0, jnp.where(row == dest_ref[1:2, :], 1.0, 0.0))
    buf_ref[...] = jnp.dot(pick.astype(BF16), x_ref[...].astype(BF16),
                           preferred_element_type=F32).astype(BF16)

    def copy(local, dest):
        return pltpu.make_async_copy(buf_ref.at[pl.ds(local, RUN_ROWS)], xs_ref.at[pl.ds(dest, RUN_ROWS)], sem)

    _start_runs(runs_ref, copy)
    _wait_runs(runs_ref, copy)


def _dispatch(x, plan, tile_rows, xs):
    n, d = x.shape
    n_tiles = n // tile_rows
    sorted_rows = xs.shape[0]
    return pl.pallas_call(
        _dispatch_kernel,
        grid=(n_tiles,),
        in_specs=[pl.BlockSpec((None, 1, 3 * N_EXPERTS), lambda i: (i, 0, 0), memory_space=pltpu.SMEM),
                  pl.BlockSpec((None, 2, tile_rows), lambda i: (i, 0, 0)),
                  pl.BlockSpec((tile_rows, d), lambda i: (i, 0)),
                  pl.BlockSpec(memory_space=pl.ANY)],
        out_specs=pl.BlockSpec(memory_space=pl.ANY),
        out_shape=jax.ShapeDtypeStruct((sorted_rows, d), BF16),
        scratch_shapes=[pltpu.VMEM((_compact_rows(tile_rows), d), BF16), pltpu.SemaphoreType.DMA(())],
        input_output_aliases={3: 0},
        name="moe_dispatch",
        compiler_params=_cparams(("arbitrary",)),
    )(plan.runs, plan.local_dest.reshape(n_tiles, 2, tile_rows), x, xs)


def _expert_kernel(te_ref, used_ref, xs_ref, wgu_ref, wd_ref, o_ref):
    del te_ref
    t = pl.program_id(0)
    d_ff = wd_ref.shape[0]
    piece = d_ff // FFN_SPLIT

    @pl.when(t < used_ref[0])
    def _():
        xb = xs_ref[...]
        acc = None
        for j in range(FFN_SPLIT):
            part = _swiglu_piece(xb, wgu_ref[:, j * piece:(j + 1) * piece],
                                 wgu_ref[:, d_ff + j * piece:d_ff + (j + 1) * piece],
                                 wd_ref[j * piece:(j + 1) * piece, :])
            acc = part if acc is None else acc + part
        o_ref[...] = acc.astype(o_ref.dtype)

    @pl.when(t >= used_ref[0])
    def _():
        o_ref[...] = jnp.zeros(o_ref.shape, o_ref.dtype)


def _experts(xs, plan, w_gu, w_down):
    r, d = xs.shape
    layer = w_gu.layer
    row_spec = pl.BlockSpec((EXPERT_ROWS, d), lambda t, te, used: (t, 0))
    grid_spec = pltpu.PrefetchScalarGridSpec(
        num_scalar_prefetch=2,
        grid=(r // EXPERT_ROWS,),
        in_specs=[row_spec,
                  pl.BlockSpec((None, None) + w_gu.shape[1:], lambda t, te, used: (layer, te[t], 0, 0)),
                  pl.BlockSpec((None, None) + w_down.shape[1:], lambda t, te, used: (layer, te[t], 0, 0))],
        out_specs=row_spec,
    )
    return pl.pallas_call(
        _expert_kernel,
        grid_spec=grid_spec,
        out_shape=jax.ShapeDtypeStruct((r, d), BF16),
        name="moe_experts",
        compiler_params=_cparams(("arbitrary",)),
    )(plan.tile_expert, plan.n_used, xs, w_gu.array, w_down.array)


def _combine_kernel(runs_ref, next_runs_ref, x_ref, dest_ref, gate_ref, ys_ref, g_ref, b_ref, o_ref,
                    buf_ref, sem, *, alpha):
    i = pl.program_id(0)
    half = lax.rem(i, 2)

    def copy_into(which):
        def copy(local, src):
            return pltpu.make_async_copy(ys_ref.at[pl.ds(src, RUN_ROWS)],
                                         buf_ref.at[which, pl.ds(local, RUN_ROWS)], sem.at[which])
        return copy

    @pl.when(i == 0)
    def _():
        buf_ref[...] = jnp.zeros(buf_ref.shape, buf_ref.dtype)
        _start_runs(runs_ref, copy_into(0))

    @pl.when(i + 1 < pl.num_programs(0))
    def _():
        _start_runs(next_runs_ref, copy_into(1 - half))

    _wait_runs(runs_ref, copy_into(half))
    y = buf_ref[half]
    col = lax.broadcasted_iota(jnp.int32, (x_ref.shape[0], y.shape[0]), 1)
    gates = gate_ref[...]
    mixed = None
    for slot in range(2):
        pick = jnp.where(col == dest_ref[:, slot:slot + 1], 1.0, 0.0).astype(BF16)
        term = gates[:, slot:slot + 1] * jnp.dot(pick, y, preferred_element_type=F32)
        mixed = term if mixed is None else mixed + term
    o_ref[...] = _layer_norm(alpha * x_ref[...] + mixed, g_ref[...], b_ref[...])


def _combine(x, plan, gates_t, ys, ln_g, ln_b, *, tile_rows, alpha):
    n, d = x.shape
    n_tiles = n // tile_rows
    row_spec = pl.BlockSpec((tile_rows, d), lambda i: (i, 0))
    pair_spec = pl.BlockSpec((tile_rows, 2), lambda i: (i, 0))
    dest_cols = plan.local_dest.reshape(n_tiles, 2, tile_rows).transpose(0, 2, 1).reshape(n, 2)
    runs_shape = (None, 1, 3 * N_EXPERTS)
    return pl.pallas_call(
        functools.partial(_combine_kernel, alpha=alpha),
        grid=(n_tiles,),
        in_specs=[pl.BlockSpec(runs_shape, lambda i: (i, 0, 0), memory_space=pltpu.SMEM),
                  pl.BlockSpec(runs_shape, lambda i: (jnp.minimum(i + 1, n_tiles - 1), 0, 0),
                               memory_space=pltpu.SMEM),
                  row_spec, pair_spec, pair_spec,
                  pl.BlockSpec(memory_space=pl.ANY),
                  _resident(ln_g), _resident(ln_b)],
        out_specs=row_spec,
        out_shape=jax.ShapeDtypeStruct((n, d), F32),
        scratch_shapes=[pltpu.VMEM((2, _compact_rows(tile_rows), d), ys.dtype),
                        pltpu.SemaphoreType.DMA((2,))],
        name="moe_combine",
        compiler_params=_cparams(("arbitrary",)),
    )(plan.runs, plan.runs, x, dest_cols, gates_t, ys, ln_g, ln_b)


def _moe_layer(streams, w_router_t, w_gu, w_down, ln_g, ln_b, *, alpha):
    d = streams[0].shape[1]
    tile_rows = min([MOVE_ROWS] + [x.shape[0] for x in streams])
    assert all(x.shape[0] % tile_rows == 0 for x in streams)
    routed = [_router(x, w_router_t) for x in streams]
    plan = _route_plan(jnp.concatenate([idx for idx, _ in routed], axis=1), tile_rows)
    tile_lo, parts = 0, []
    for x in streams:
        tile_hi = tile_lo + x.shape[0] // tile_rows
        parts.append(plan._replace(local_dest=plan.local_dest[tile_lo:tile_hi], runs=plan.runs[tile_lo:tile_hi]))
        tile_lo = tile_hi
    xs = jnp.zeros((plan.tile_expert.shape[0] * EXPERT_ROWS, d), BF16)
    for x, part in zip(streams, parts):
        xs = _dispatch(x, part, tile_rows, xs)
    ys = _experts(xs, plan, w_gu, w_down)
    return [_combine(x, part, gates.T, ys, ln_g, ln_b, tile_rows=tile_rows, alpha=alpha)
            for x, part, (_, gates) in zip(streams, parts, routed)]


def _kv_kernel(x_ref, w_ref, wvt_ref, k_ref, v_ref, kb_ref, vt_ref):
    qk_w = kb_ref.shape[-1]
    xb = x_ref[...].astype(BF16)
    kv = jnp.dot(xb, w_ref[...], preferred_element_type=F32)
    k = kv[:, :qk_w]
    k_ref[...] = k
    v_ref[...] = kv[:, qk_w:]
    kb_ref[...] = k.astype(BF16)
    vt_ref[...] = _dot_nt(wvt_ref[...], xb).astype(BF16)


def _kv_proj(x, w_kv, w_vt):
    bsz, t, d = x.shape
    qk_w = w_kv.shape[1] // 2
    rows = min(ATT_K, t)
    assert t % rows == 0
    row_spec = pl.BlockSpec((None, rows, qk_w), lambda b, i: (b, i, 0))
    return pl.pallas_call(
        _kv_kernel,
        grid=(bsz, t // rows),
        in_specs=[pl.BlockSpec((None, rows, d), lambda b, i: (b, i, 0)),
                  _resident(w_kv), _resident(w_vt)],
        out_specs=[row_spec, row_spec, row_spec,
                   pl.BlockSpec((None, None, qk_w, rows), lambda b, i: (b, i, 0, 0))],
        out_shape=[jax.ShapeDtypeStruct((bsz, t, qk_w), F32),
                   jax.ShapeDtypeStruct((bsz, t, qk_w), F32),
                   jax.ShapeDtypeStruct((bsz, t, qk_w), BF16),
                   jax.ShapeDtypeStruct((bsz, t // rows, qk_w, rows), BF16)],
        name="kv_proj",
        compiler_params=_cparams(("parallel", "parallel")),
    )(x, w_kv, w_vt)


def _t5_bucket(rel):
    nb = NUM_BUCKETS // 2
    max_exact = nb // 2
    ret = (rel > 0).astype(jnp.int32) * nb
    n = jnp.abs(rel)
    nf = jnp.maximum(n, 1).astype(jnp.float32)
    large = max_exact + (jnp.log(nf / max_exact) / math.log(MAX_DISTANCE / max_exact)
                         * (nb - max_exact)).astype(jnp.int32)
    large = jnp.minimum(large, nb - 1)
    return ret + jnp.where(n < max_exact, n, large)


def _bucket_table(q_pos, k_pos, visible=None):
    bucket = _t5_bucket(k_pos[:, None] - q_pos[None, :])
    return bucket if visible is None else jnp.where(visible, bucket, -1)


def _bias_kernel(far_ref, rb_ref, bucket_ref, o_ref):
    h = pl.program_id(0)
    bucket = bucket_ref[...]
    far = rb_ref[far_ref[0], h]
    out = jnp.zeros(bucket.shape, F32)
    for b in range(NUM_BUCKETS):
        out = jnp.where(bucket == b, rb_ref[b, h] - far, out)
    o_ref[...] = jnp.where(bucket < 0, NEG_INF, out * LOG2_E)


def _bias_tables(rel_bias, buckets):
    n_heads2 = rel_bias.shape[1]
    far_bucket = _t5_bucket(jnp.full((1,), -MAX_DISTANCE, jnp.int32))
    return pl.pallas_call(
        _bias_kernel,
        grid=(n_heads2,),
        in_specs=[pl.BlockSpec(memory_space=pltpu.SMEM), pl.BlockSpec(memory_space=pltpu.SMEM),
                  _resident(buckets)],
        out_specs=pl.BlockSpec((None,) + buckets.shape, lambda h: (h, 0, 0)),
        out_shape=jax.ShapeDtypeStruct((n_heads2,) + buckets.shape, F32),
        name="bias_tables",
        compiler_params=_cparams(("parallel",)),
    )(far_bucket, rel_bias.astype(F32), buckets)


def _att_init(x_ref, wqt_ref, qa_ref, qb_ref, m_ref, l_ref, acc_ref):
    qt = _dot_nt(wqt_ref[...], x_ref[...].astype(BF16)) * (HEAD_DIM ** -0.5 * LOG2_E)
    row = lax.broadcasted_iota(jnp.int32, (LANES, qt.shape[1]), 0)
    for h in range(qa_ref.shape[0]):
        slab = qt[h * LANES:(h + 1) * LANES, :]
        qa_ref[h] = jnp.where(row < HEAD_DIM, slab, 0.0).astype(BF16)
        qb_ref[h] = jnp.where(row >= HEAD_DIM, slab, 0.0).astype(BF16)
    m_ref[...] = jnp.full(m_ref.shape, NEG_INF, F32)
    l_ref[...] = jnp.zeros(l_ref.shape, F32)
    acc_ref[...] = jnp.zeros(acc_ref.shape, F32)


def _att_block(qa_ref, qb_ref, k, vt_slab, bias, m_ref, l_ref, acc_ref):
    n_heads = 2 * qa_ref.shape[0]

    def scores(hh):
        h = hh // 2
        q_ref = qa_ref if hh % 2 == 0 else qb_ref
        s = jnp.dot(k[:, h * LANES:(h + 1) * LANES], q_ref[h], preferred_element_type=F32)
        return s if bias is None else s + bias(hh)

    s_of, p_of, scale_of = {}, {}, {}

    def softmax(hh):
        s = s_of.pop(hh)
        m_old = m_ref[hh]
        m_new = jnp.maximum(m_old, jnp.max(s, axis=0, keepdims=True))
        scale = jnp.exp2(m_old - m_new)
        p = jnp.exp2(s - m_new)
        l_ref[hh] = scale * l_ref[hh] + jnp.sum(p, axis=0, keepdims=True)
        m_ref[hh] = m_new
        p_of[hh] = p.astype(BF16)
        scale_of[hh] = scale

    def accumulate(hh):
        acc_ref[hh] = scale_of.pop(hh) * acc_ref[hh] + jnp.dot(vt_slab(hh // 2), p_of.pop(hh),
                                                               preferred_element_type=F32)

    n_groups = n_heads // HEAD_GROUP
    group = lambda t: range(t * HEAD_GROUP, (t + 1) * HEAD_GROUP)
    for t in range(-2, n_groups):
        if t + 2 < n_groups:
            for hh in group(t + 2):
                s_of[hh] = scores(hh)
        if 0 <= t + 1 < n_groups:
            for hh in group(t + 1):
                softmax(hh)
        if t >= 0:
            for hh in group(t):
                accumulate(hh)


def _att_finish(x_ref, lam_ref, sg_ref, wout_ref, g_ref, b_ref, o_ref, l_ref, acc_ref, heads_ref,
                *, lam_init, alpha):
    lp = lam_ref[...]
    lam = (jnp.exp(jnp.sum(lp[0:1] * lp[1:2], axis=-1, keepdims=True))
           - jnp.exp(jnp.sum(lp[2:3] * lp[3:4], axis=-1, keepdims=True)) + lam_init)
    for h in range(acc_ref.shape[0] // 2):
        a = acc_ref[2 * h] * (1.0 / l_ref[2 * h]) - (lam / l_ref[2 * h + 1]) * acc_ref[2 * h + 1]
        r = a * lax.rsqrt(jnp.mean(a * a, axis=0, keepdims=True) + LN_EPS) * sg_ref[...]
        heads_ref[h * LANES:(h + 1) * LANES, :] = r * (1.0 - lam_init)
    y = lax.dot_general(heads_ref[...].astype(BF16), wout_ref[...], (((0,), (0,)), ((), ())),
                        preferred_element_type=F32)
    o_ref[...] = _layer_norm(alpha * x_ref[...] + y, g_ref[...], b_ref[...])


def _att_scratch(q_rows, width):
    n_slabs = width // LANES
    return [pltpu.VMEM((n_slabs, LANES, q_rows), BF16), pltpu.VMEM((n_slabs, LANES, q_rows), BF16),
            pltpu.VMEM((2 * n_slabs, 1, q_rows), F32), pltpu.VMEM((2 * n_slabs, 1, q_rows), F32),
            pltpu.VMEM((2 * n_slabs, LANES, q_rows), F32), pltpu.VMEM((width, q_rows), F32)]


STEP_FAR_WIDE = 0
STEP_FAR_ONE = 1
STEP_NEAR = 2
FAR_TILES = 4


def _prompt_att_kernel(qi_ref, kind_ref, first_ref, wide_ref, ka_ref_idx, kb_ref_idx,
                       x_ref, kw_ref, vtw_ref, ka_ref, vta_ref, kb_ref, vtb_ref, wqt_ref, bias_ref, lam_ref,
                       sg_ref, wout_ref, g_ref, b_ref, o_ref, qa_ref, qb_ref, m_ref, l_ref, acc_ref,
                       heads_ref, *, lam_init, alpha):
    del wide_ref, ka_ref_idx, kb_ref_idx
    p = pl.program_id(1)
    kind = kind_ref[p]
    tk = ka_ref.shape[0]
    state = (m_ref, l_ref, acc_ref)

    def slab(ref, h):
        return ref[h * LANES:(h + 1) * LANES, :]

    @pl.when(first_ref[p] == 1)
    def _():
        _att_init(x_ref, wqt_ref, qa_ref, qb_ref, *state)

    @pl.when(kind == STEP_FAR_WIDE)
    def _():
        def vt_slab(h):
            return jnp.concatenate([vtw_ref[c, h * LANES:(h + 1) * LANES, :]
                                    for c in range(vtw_ref.shape[0])], axis=1)
        _att_block(qa_ref, qb_ref, kw_ref[...], vt_slab, None, *state)

    @pl.when(kind == STEP_FAR_ONE)
    def _():
        _att_block(qa_ref, qb_ref, ka_ref[...], lambda h: slab(vta_ref, h), None, *state)

    @pl.when(kind == STEP_NEAR)
    def _():
        hide = jnp.where(qi_ref[p] == 0, NEG_INF, 0.0)

        def bias(hh):
            return jnp.concatenate([bias_ref[hh, 0:tk, :] + hide, bias_ref[hh, tk:2 * tk, :]], axis=0)

        def vt_slab(h):
            return jnp.concatenate([slab(vta_ref, h), slab(vtb_ref, h)], axis=1)

        k = jnp.concatenate([ka_ref[...], kb_ref[...]], axis=0)
        _att_block(qa_ref, qb_ref, k, vt_slab, bias, *state)
        _att_finish(x_ref, lam_ref, sg_ref, wout_ref, g_ref, b_ref, o_ref, l_ref, acc_ref, heads_ref,
                    lam_init=lam_init, alpha=alpha)


def _prompt_bias(rel_bias, tq):
    pos = jnp.arange(tq, dtype=jnp.int32)
    visible = pos[:, None] < (pos[None, :] // CHUNK + 1) * CHUNK
    buckets = jnp.concatenate([_bucket_table(pos + tq, pos), _bucket_table(pos, pos, visible)], axis=0)
    return _bias_tables(rel_bias, buckets)


def _prompt_steps(n_q):
    steps = []
    wide = ka = kb = 0
    for i in range(n_q):
        n_far = max(i - 1, 0)
        first = 1
        for c in range(n_far // FAR_TILES):
            wide = c
            steps.append((i, STEP_FAR_WIDE, first, wide, ka, kb))
            first = 0
        for j in range(n_far - n_far % FAR_TILES, n_far):
            ka = j
            steps.append((i, STEP_FAR_ONE, first, wide, ka, kb))
            first = 0
        ka, kb = max(i - 1, 0), i
        steps.append((i, STEP_NEAR, first, wide, ka, kb))
    return [jnp.array(col, jnp.int32) for col in zip(*steps)]


def _prompt_attention(x, kb, vt, bias, w_qt, lam_p, subln_g, w_out, ln_g, ln_b, *, lam_init, alpha):
    bsz, s_len, d = x.shape
    width = kb.shape[-1]
    tq = vt.shape[-1]
    n_q = s_len // tq
    assert s_len % tq == 0 and tq % CHUNK == 0 and tq >= MAX_DISTANCE
    assert n_q % FAR_TILES == 0 or n_q <= FAR_TILES
    steps = _prompt_steps(n_q)
    wide_tiles = min(FAR_TILES, n_q)
    x_spec = pl.BlockSpec((None, tq, d), lambda b, p, qi, kind, first, wide, ka, kb: (b, qi[p], 0))
    grid_spec = pltpu.PrefetchScalarGridSpec(
        num_scalar_prefetch=len(steps),
        grid=(bsz, steps[0].shape[0]),
        in_specs=[x_spec,
                  pl.BlockSpec((None, wide_tiles * tq, width),
                               lambda b, p, qi, kind, first, wide, ka, kb: (b, wide[p], 0)),
                  pl.BlockSpec((None, wide_tiles, width, tq),
                               lambda b, p, qi, kind, first, wide, ka, kb: (b, wide[p], 0, 0)),
                  pl.BlockSpec((None, tq, width), lambda b, p, qi, kind, first, wide, ka, kb: (b, ka[p], 0)),
                  pl.BlockSpec((None, None, width, tq),
                               lambda b, p, qi, kind, first, wide, ka, kb: (b, ka[p], 0, 0)),
                  pl.BlockSpec((None, tq, width), lambda b, p, qi, kind, first, wide, ka, kb: (b, kb[p], 0)),
                  pl.BlockSpec((None, None, width, tq),
                               lambda b, p, qi, kind, first, wide, ka, kb: (b, kb[p], 0, 0)),
                  _resident(w_qt), _resident(bias), _resident(lam_p), _resident(subln_g), _resident(w_out),
                  _resident(ln_g), _resident(ln_b)],
        out_specs=x_spec,
        scratch_shapes=_att_scratch(tq, width),
    )
    return pl.pallas_call(
        functools.partial(_prompt_att_kernel, lam_init=lam_init, alpha=alpha),
        grid_spec=grid_spec,
        out_shape=jax.ShapeDtypeStruct((bsz, s_len, d), F32),
        name="prompt_attention",
        compiler_params=_cparams(("parallel", "arbitrary")),
    )(*steps, x, kb, vt, kb, vt, kb, vt, _operand(w_qt), bias, lam_p, subln_g, _operand(w_out), ln_g, ln_b)


def _sample_att_kernel(x_ref, ck_ref, cv_ref, kn_ref, vtn_ref, wqt_ref, bias_ref, lam_ref, sg_ref,
                       wout_ref, g_ref, b_ref, o_ref, qa_ref, qb_ref, m_ref, l_ref, acc_ref, heads_ref,
                       *, lam_init, alpha):
    j = pl.program_id(1)
    n_cache = pl.num_programs(1) - 1
    tk = ck_ref.shape[0]
    t_new = kn_ref.shape[0]
    state = (m_ref, l_ref, acc_ref)

    def near_bias(hh):
        return bias_ref[hh, 0:tk, :]

    def new_bias(hh):
        return bias_ref[hh, tk:tk + t_new, :]

    def cached_vt_slab(h):
        return jnp.transpose(cv_ref[:, h * LANES:(h + 1) * LANES]).astype(BF16)

    def new_vt_slab(h):
        return vtn_ref[h * LANES:(h + 1) * LANES, :]

    @pl.when(j == 0)
    def _():
        _att_init(x_ref, wqt_ref, qa_ref, qb_ref, *state)

    @pl.when(j < n_cache - 1)
    def _():
        _att_block(qa_ref, qb_ref, ck_ref[...].astype(BF16), cached_vt_slab, None, *state)

    @pl.when(j == n_cache - 1)
    def _():
        _att_block(qa_ref, qb_ref, ck_ref[...].astype(BF16), cached_vt_slab, near_bias, *state)

    @pl.when(j == n_cache)
    def _():
        _att_block(qa_ref, qb_ref, kn_ref[...], new_vt_slab, new_bias, *state)
        _att_finish(x_ref, lam_ref, sg_ref, wout_ref, g_ref, b_ref, o_ref, l_ref, acc_ref, heads_ref,
                    lam_init=lam_init, alpha=alpha)


def _sample_bias(rel_bias, past, t):
    tk = min(CACHE_K, past)
    q_pos = past + jnp.arange(t, dtype=jnp.int32)
    buckets = jnp.concatenate([_bucket_table(q_pos, past - tk + jnp.arange(tk, dtype=jnp.int32)),
                               _bucket_table(q_pos, q_pos)], axis=0)
    return _bias_tables(rel_bias, buckets)


def _sample_attention(x, cache_k, cache_v, kb, vt, bias, w_qt, lam_p, subln_g, w_out, ln_g, ln_b,
                      *, lam_init, alpha):
    bsz, t, d = x.shape
    past = cache_k.shape[1]
    width = kb.shape[-1]
    tk = min(CACHE_K, past)
    assert past % tk == 0 and tk >= MAX_DISTANCE and vt.shape[1] == 1
    n_cache = past // tk
    cache_spec = pl.BlockSpec((None, tk, width), lambda b, j: (b, jnp.minimum(j, n_cache - 1), 0))
    x_spec = pl.BlockSpec((None, t, d), lambda b, j: (b, 0, 0))
    return pl.pallas_call(
        functools.partial(_sample_att_kernel, lam_init=lam_init, alpha=alpha),
        grid=(bsz, n_cache + 1),
        in_specs=[x_spec, cache_spec, cache_spec,
                  pl.BlockSpec((None, t, width), lambda b, j: (b, 0, 0)),
                  pl.BlockSpec((None, None, width, t), lambda b, j: (b, 0, 0, 0)),
                  _resident(w_qt), _resident(bias), _resident(lam_p), _resident(subln_g), _resident(w_out),
                  _resident(ln_g), _resident(ln_b)],
        out_specs=x_spec,
        out_shape=jax.ShapeDtypeStruct((bsz, t, d), F32),
        scratch_shapes=_att_scratch(t, width),
        name="sample_attention",
        compiler_params=_cparams(("parallel", "arbitrary")),
    )(x, cache_k, cache_v, kb, vt, _operand(w_qt), bias, lam_p, subln_g, _operand(w_out), ln_g, ln_b)


class _Stream:
    def __init__(self, x, cache_k, cache_v, emit_v):
        self.bsz, self.t, self.d = x.shape
        self.h = x.reshape(self.bsz * self.t, self.d)
        self.cache_k, self.cache_v, self.emit_v = cache_k, cache_v, emit_v
        self.sgu_rows = []
        self.k_new = self.v_new = self.kb = self.vt = self.bias = None


def _mixer(s, i, w, *, depth, alpha):
    n_a = depth // 2
    if i < n_a:
        chunk = min(s.t, SGU_CHUNK)
        s.h, v_rows = _sgu_layer(
            s.h, _Stacked(w["a_w_in"], i), w["a_ln_g"][i], w["a_ln_b"][i], w["a_w_s"][i][:, :chunk, :chunk],
            w["a_b_s"][i][:, :chunk].T, _Stacked(w["a_w_out"], i), w["ln_mix_g"][i], w["ln_mix_b"][i],
            chunk=chunk, alpha=alpha, emit_v=s.emit_v)
        s.sgu_rows.append(v_rows)
        return
    h3 = s.h.reshape(s.bsz, s.t, s.d)
    if s.k_new is None:
        s.k_new, s.v_new, s.kb, s.vt = _kv_proj(h3, w["w_kv"], w["w_vt"])
        if s.cache_k is None:
            s.bias = _prompt_bias(w["rel_bias"], s.vt.shape[-1])
        else:
            s.bias = _sample_bias(w["rel_bias"], s.cache_k.shape[1], s.t)
    j = i - n_a
    lam_init = 0.8 - 0.6 * math.exp(-0.3 * i)
    args = (s.kb, s.vt, s.bias, _Stacked(w["b_w_qt"], j), w["b_lam"][j], w["b_subln_g"][j],
            _Stacked(w["b_w_out"], j), w["ln_mix_g"][i], w["ln_mix_b"][i])
    if s.cache_k is None:
        out = _prompt_attention(h3, *args, lam_init=lam_init, alpha=alpha)
    else:
        past = s.cache_k.shape[1]
        out = _sample_attention(h3, s.cache_k.reshape(s.bsz, past, -1), s.cache_v.reshape(s.bsz, past, -1),
                                *args, lam_init=lam_init, alpha=alpha)
    s.h = out.reshape(s.bsz * s.t, s.d)


def _run_layers(streams, w, *, depth):
    alpha = (2 * depth) ** 0.25
    for i in range(depth):
        for s in streams:
            _mixer(s, i, w, depth=depth, alpha=alpha)
        if i % 2 == 0:
            for s in streams:
                s.h = _ffn_layer(s.h, _Stacked(w["ffn_w_gu"], i // 2), _Stacked(w["ffn_w_down"], i // 2),
                                 w["ln_ffn_g"][i], w["ln_ffn_b"][i], alpha=alpha)
        else:
            outs = _moe_layer([s.h for s in streams], w["moe_w_router_t"][i // 2],
                              _Stacked(w["moe_w_gu"], i // 2), _Stacked(w["moe_w_down"], i // 2),
                              w["ln_ffn_g"][i], w["ln_ffn_b"][i], alpha=alpha)
            for s, out in zip(streams, outs):
                s.h = out


def kernel(x_prompt, x_sample, cache_k, cache_v, a_w_in, a_ln_g, a_ln_b, a_w_s, a_b_s, a_w_out, w_kv, b_w_q,
           b_lam, b_subln_g, b_w_out, rel_bias, ln_mix_g, ln_mix_b, ln_ffn_g, ln_ffn_b, ffn_w_gu, ffn_w_down,
           moe_w_router, moe_w_gu, moe_w_down):
    depth = ln_mix_g.shape[0]
    row = lambda a: a[:, None, :]
    w = dict(
        a_w_in=a_w_in.astype(BF16), a_ln_g=row(a_ln_g), a_ln_b=row(a_ln_b), a_w_s=a_w_s, a_b_s=a_b_s,
        a_w_out=a_w_out.astype(BF16), w_kv=w_kv.astype(BF16),
        w_vt=jnp.transpose(w_kv[:, w_kv.shape[1] // 2:]).astype(BF16),
        b_w_qt=jnp.swapaxes(b_w_q, 1, 2).astype(BF16), b_lam=b_lam,
        b_subln_g=b_subln_g[:, :, None], b_w_out=b_w_out.astype(BF16), rel_bias=rel_bias,
        ln_mix_g=row(ln_mix_g), ln_mix_b=row(ln_mix_b), ln_ffn_g=row(ln_ffn_g), ln_ffn_b=row(ln_ffn_b),
        ffn_w_gu=ffn_w_gu.astype(BF16), ffn_w_down=ffn_w_down.astype(BF16),
        moe_w_router_t=jnp.swapaxes(moe_w_router, 1, 2), moe_w_gu=moe_w_gu.astype(BF16),
        moe_w_down=moe_w_down.astype(BF16))
    prompt = _Stream(x_prompt, None, None, emit_v=False)
    sample = _Stream(x_sample, cache_k, cache_v, emit_v=True)
    _run_layers([prompt, sample], w, depth=depth)
    d_sgu = a_w_out.shape[1]
    k_heads = (rel_bias.shape[1], HEAD_DIM)
    v_heads = (rel_bias.shape[1] // 2, 2 * HEAD_DIM)
    return (prompt.h.reshape(x_prompt.shape), sample.h.reshape(x_sample.shape),
            prompt.k_new.reshape(x_prompt.shape[:2] + k_heads), prompt.v_new.reshape(x_prompt.shape[:2] + v_heads),
            sample.k_new.reshape(x_sample.shape[:2] + k_heads), sample.v_new.reshape(x_sample.shape[:2] + v_heads),
            jnp.stack([r.reshape(x_sample.shape[:2] + (d_sgu,)) for r in sample.sgu_rows], axis=0))
```
